```python
import math, functools
import jax, jax.numpy as jnp
from jax import lax
import numpy as np

D_MODEL = 1024
BATCH = 2
SEQ = 8192
DEPTH = 2
DEC_BATCH = 32
DEC_SEQ = 1
PAST_LEN = 8192
PAGE_SIZE = 128

MIX_W = D_MODEL
POOL_W = MIX_W // 2
POOL_WINDOWS = (2, 4, 8, 16)
POOL_GROUPS = len(POOL_WINDOWS)
POOL_GC = POOL_W // POOL_GROUPS
POOL_HIST = max(POOL_WINDOWS) - 1
ATTN_W = MIX_W - POOL_W
HEAD_DIM = 64
VDIM = 2 * HEAD_DIM
N_HEADS = ATTN_W // VDIM
IN_W = POOL_W + 3 * ATTN_W
D_FF = 2816
CONV_W = 3
ROPE_THETA = 10000.0
Q_BLOCK = 128
EPS = 1e-6

kernel_name = "hymba_pool_diffattn_convffn_step"

F32 = jnp.float32


def rms_norm(x, g):
    xf = x.astype(F32)
    y = xf * lax.rsqrt(jnp.mean(xf * xf, axis=-1, keepdims=True) + EPS)
    return (y * g.astype(F32)).astype(x.dtype)


def lambda_init(layer):
    return 0.8 - 0.6 * math.exp(-0.3 * layer)


def rope(x, pos):
    half = HEAD_DIM // 2
    inv = 1.0 / (ROPE_THETA ** (jnp.arange(half, dtype=F32) * (2.0 / HEAD_DIM)))
    ang = pos.astype(F32)[:, None] * inv[None, :]
    cos = jnp.cos(ang)[None, :, None, None, :]
    sin = jnp.sin(ang)[None, :, None, None, :]
    xf = x.astype(F32)
    x1, x2 = xf[..., :half], xf[..., half:]
    return jnp.concatenate([x1 * cos - x2 * sin, x2 * cos + x1 * sin], axis=-1).astype(x.dtype)


def pool_mixer(u, prev, pos, pool_w, pool_scale):
    B, T, _ = u.shape
    ext = jnp.concatenate([prev, u], axis=1)
    c = jnp.cumsum(ext.astype(F32), axis=1)
    c = jnp.concatenate([jnp.zeros((B, 1, POOL_W), F32), c], axis=1)
    cg = c.reshape(B, POOL_HIST + T + 1, POOL_GROUPS, POOL_GC)
    end = cg[:, POOL_HIST + 1:]
    means = []
    for g, w in enumerate(POOL_WINDOWS):
        start = cg[:, POOL_HIST + 1 - w: POOL_HIST + 1 - w + T, g]
        cnt = jnp.minimum(pos + 1, w).astype(F32)[None, :, None]
        means.append((end[:, :, g] - start) / cnt)
    mean = jnp.stack(means, axis=2)
    ug = u.astype(F32).reshape(B, T, POOL_GROUPS, POOL_GC)
    d = (mean - ug).astype(u.dtype)
    y = jnp.einsum('btgc,gcd->btgd', d, pool_w) * pool_scale.reshape(POOL_GROUPS, POOL_GC)
    return y.reshape(B, T, POOL_W), ext[:, -POOL_HIST:]


def diff_core(q, k, v, q_pos, k_pos, lam):
    s = jnp.einsum('bqhcd,bkhcd->bhcqk', q, k).astype(F32) * (HEAD_DIM ** -0.5)
    mask = k_pos[None, :] <= q_pos[:, None]
    s = jnp.where(mask, s, -jnp.inf)
    p = jax.nn.softmax(s, axis=-1)
    a = p[:, :, 0] - lam * p[:, :, 1]
    return jnp.einsum('bhqk,bkhe->bqhe', a.astype(v.dtype), v)


def prompt_attention(q, k, v, lam, *, pos):
    B, S = q.shape[0], q.shape[1]
    nb = S // Q_BLOCK
    qb = jnp.moveaxis(q.reshape(B, nb, Q_BLOCK, N_HEADS, 2, HEAD_DIM), 1, 0)
    pb = pos.reshape(nb, Q_BLOCK)
    ob = lax.map(lambda a: diff_core(a[0], k, v, a[1], pos, lam), (qb, pb))
    return jnp.moveaxis(ob, 0, 1).reshape(B, S, N_HEADS, VDIM)


def sample_attention(q, k, v, lam, *, pos, cache_k_l, cache_v_l, page_table):
    DB, T = q.shape[0], q.shape[1]
    past = page_table.shape[1] * cache_k_l.shape[1]
    k_past = cache_k_l[page_table].reshape(DB, past, N_HEADS, 2, HEAD_DIM)
    v_past = cache_v_l[page_table].reshape(DB, past, N_HEADS, VDIM)
    k_all = jnp.concatenate([k_past, k.astype(k_past.dtype)], axis=1)
    v_all = jnp.concatenate([v_past, v.astype(v_past.dtype)], axis=1)
    k_pos = jnp.arange(past + T, dtype=jnp.int32)
    return diff_core(q, k_all, v_all, pos, k_pos, lam)


def conv_ffn(h, prev, w_up, conv_w, conv_b, w_down):
    up = h @ w_up
    a, b = up[..., :D_FF], up[..., D_FF:]
    ext = jnp.concatenate([prev, a], axis=1)
    c = lax.conv_general_dilated(ext, conv_w[:, None, :].astype(ext.dtype), window_strides=(1,),
                                 padding='VALID', dimension_numbers=('NWC', 'WIO', 'NWC'),
                                 feature_group_count=D_FF) + conv_b
    return (jax.nn.silu(c) * b) @ w_down, ext[:, -(CONV_W - 1):]


def layer(x, pos, prev_pool, prev_conv, attend, lam_init, p):
    B, T, _ = x.shape
    h = rms_norm(x, p['norm1_g'])
    proj = h @ p['w_in']
    u = proj[..., :POOL_W]
    q = proj[..., POOL_W:POOL_W + ATTN_W].reshape(B, T, N_HEADS, 2, HEAD_DIM)
    k = proj[..., POOL_W + ATTN_W:POOL_W + 2 * ATTN_W].reshape(B, T, N_HEADS, 2, HEAD_DIM)
    v = proj[..., POOL_W + 2 * ATTN_W:].reshape(B, T, N_HEADS, VDIM)
    q, k = rope(q, pos), rope(k, pos)
    pool_y, new_pool = pool_mixer(u, prev_pool, pos, p['pool_w'], p['pool_scale'])
    lam = (jnp.exp(jnp.sum(p['lam_q1'].astype(F32) * p['lam_k1'].astype(F32)))
           - jnp.exp(jnp.sum(p['lam_q2'].astype(F32) * p['lam_k2'].astype(F32))) + lam_init)
    o = attend(q, k, v, lam)
    attn_y = (rms_norm(o, p['subln_g']) * (1.0 - lam_init)).reshape(B, T, ATTN_W)
    x = x + jnp.concatenate([pool_y, attn_y.astype(pool_y.dtype)], axis=-1) @ p['w_out']
    f, new_conv = conv_ffn(rms_norm(x, p['norm2_g']), prev_conv, p['w_up'], p['conv_w'],
                           p['conv_b'], p['w_down'])
    return x + f, k, v, new_pool, new_conv


def setup_inputs(seed: int = 0) -> dict:
    key = jax.random.key(seed)
    ks = jax.random.split(key, 24)
    n = lambda i, shape: jax.random.normal(ks[i], shape, F32)
    n_pages = PAST_LEN // PAGE_SIZE
    n_used = DEC_BATCH * n_pages
    n_pool = n_used + max(1, n_used // 4)
    page_table = jax.random.permutation(ks[0], n_pool)[:n_used].reshape(DEC_BATCH, n_pages).astype(jnp.int32)
    return {
        "x_prompt": n(1, (BATCH, SEQ, D_MODEL)),
        "x_sample": n(2, (DEC_BATCH, DEC_SEQ, D_MODEL)),
        "cache_k": n(3, (DEPTH, n_pool, PAGE_SIZE, N_HEADS, VDIM)),
        "cache_v": n(4, (DEPTH, n_pool, PAGE_SIZE, N_HEADS, VDIM)),
        "state_pool": n(5, (DEPTH, DEC_BATCH, POOL_HIST, POOL_W)),
        "state_conv": n(6, (DEPTH, DEC_BATCH, CONV_W - 1, D_FF)),
        "page_table": page_table,
        "norm1_g": 1.0 + 0.02 * n(7, (DEPTH, D_MODEL)),
        "w_in": n(8, (DEPTH, D_MODEL, IN_W)) * D_MODEL ** -0.5,
        "pool_w": n(9, (DEPTH, POOL_GROUPS, POOL_GC, POOL_GC)) * POOL_GC ** -0.5,
        "pool_scale": 1.0 + 0.1 * n(10, (DEPTH, POOL_W)),
        "lam_q1": 0.1 * n(11, (DEPTH, HEAD_DIM)),
        "lam_k1": 0.1 * n(12, (DEPTH, HEAD_DIM)),
        "lam_q2": 0.1 * n(13, (DEPTH, HEAD_DIM)),
        "lam_k2": 0.1 * n(14, (DEPTH, HEAD_DIM)),
        "subln_g": 1.0 + 0.02 * n(15, (DEPTH, VDIM)),
        "w_out": n(16, (DEPTH, MIX_W, D_MODEL)) * MIX_W ** -0.5,
        "norm2_g": 1.0 + 0.02 * n(17, (DEPTH, D_MODEL)),
        "w_up": n(18, (DEPTH, D_MODEL, 2 * D_FF)) * D_MODEL ** -0.5,
        "conv_w": n(19, (DEPTH, CONV_W, D_FF)) * CONV_W ** -0.5,
        "conv_b": 0.01 * n(20, (DEPTH, D_FF)),
        "w_down": n(21, (DEPTH, D_FF, D_MODEL)) * D_FF ** -0.5,
        "final_g": 1.0 + 0.02 * n(22, (D_MODEL,)),
    }


def reference(x_prompt, x_sample, cache_k, cache_v, state_pool, state_conv, page_table,
              norm1_g, w_in, pool_w, pool_scale, lam_q1, lam_k1, lam_q2, lam_k2, subln_g,
              w_out, norm2_g, w_up, conv_w, conv_b, w_down, final_g):
    B, S, _ = x_prompt.shape
    DB, T, _ = x_sample.shape
    past = page_table.shape[1] * cache_k.shape[2]
    pos_p = jnp.arange(S, dtype=jnp.int32)
    pos_s = past + jnp.arange(T, dtype=jnp.int32)
    xp, xs = x_prompt, x_sample
    kp_l, vp_l, poolp_l, convp_l = [], [], [], []
    ks_l, vs_l, pools_l, convs_l = [], [], [], []
    for l in range(DEPTH):
        p = dict(norm1_g=norm1_g[l], w_in=w_in[l], pool_w=pool_w[l], pool_scale=pool_scale[l],
                 lam_q1=lam_q1[l], lam_k1=lam_k1[l], lam_q2=lam_q2[l], lam_k2=lam_k2[l],
                 subln_g=subln_g[l], w_out=w_out[l], norm2_g=norm2_g[l], w_up=w_up[l],
                 conv_w=conv_w[l], conv_b=conv_b[l], w_down=w_down[l])
        li = lambda_init(l)
        xp, kp, vp, poolp, convp = layer(
            xp, pos_p, jnp.zeros((B, POOL_HIST, POOL_W), xp.dtype),
            jnp.zeros((B, CONV_W - 1, D_FF), xp.dtype),
            functools.partial(prompt_attention, pos=pos_p), li, p)
        xs, ksn, vsn, pools, convs = layer(
            xs, pos_s, state_pool[l], state_conv[l],
            functools.partial(sample_attention, pos=pos_s, cache_k_l=cache_k[l],
                              cache_v_l=cache_v[l], page_table=page_table), li, p)
        kp_l.append(kp.reshape(B, S // PAGE_SIZE, PAGE_SIZE, N_HEADS, VDIM))
        vp_l.append(vp.reshape(B, S // PAGE_SIZE, PAGE_SIZE, N_HEADS, VDIM))
        poolp_l.append(poolp)
        convp_l.append(convp)
        ks_l.append(ksn.reshape(DB, T, N_HEADS, VDIM))
        vs_l.append(vsn)
        pools_l.append(pools)
        convs_l.append(convs)
    y_prompt = rms_norm(xp, final_g)
    y_sample = rms_norm(xs, final_g)
    k_new_prompt = jnp.stack(kp_l)
    v_new_prompt = jnp.stack(vp_l)
    pool_new_prompt = jnp.stack(poolp_l)
    conv_new_prompt = jnp.stack(convp_l)
    k_new_sample = jnp.stack(ks_l)
    v_new_sample = jnp.stack(vs_l)
    pool_new_sample = jnp.stack(pools_l)
    conv_new_sample = jnp.stack(convs_l)
    return (y_prompt, y_sample, k_new_prompt, v_new_prompt, pool_new_prompt, conv_new_prompt,
            k_new_sample, v_new_sample, pool_new_sample, conv_new_sample)
```

```python
import functools
import math

import jax
import jax.numpy as jnp
from jax import lax
from jax.experimental import pallas as pl
from jax.experimental.pallas import tpu as pltpu

F32 = jnp.float32
BF16 = jnp.bfloat16

D_MODEL = 1024
POOL_W = 512
POOL_WINDOWS = (2, 4, 8, 16)
POOL_GC = 128
POOL_HIST = 15
ATTN_W = 512
HEAD_DIM = 64
VDIM = 128
N_HEADS = 4
IN_W = POOL_W + 3 * ATTN_W
D_FF = 2816
CONV_W = 3
ROPE_THETA = 10000.0
EPS = 1e-6
Q_SCALE = HEAD_DIM ** -0.5
NEG_BIG = -1e30

ROW_TILE = 512
FF_CHUNK = 256
N_FF_CHUNKS = D_FF // FF_CHUNK
CARRY_ROWS = 8
POOL_CARRY = 16
PAGES_PER_STEP = 8
VMEM_LIMIT = 56 * 1024 * 1024


def _lambda_init(layer):
    return 0.8 - 0.6 * math.exp(-0.3 * layer)


def _rms(x, g):
    return x * lax.rsqrt(jnp.mean(x * x, axis=-1, keepdims=True) + EPS) * g


def _rope(x, cos, sin_signed):
    lane = lax.broadcasted_iota(jnp.int32, x.shape, 1)
    rot = jnp.where((lane % HEAD_DIM) < HEAD_DIM // 2,
                    pltpu.roll(x, VDIM - HEAD_DIM // 2, 1), pltpu.roll(x, HEAD_DIM // 2, 1))
    return x * cos + rot * sin_signed


def _lam(lq1, lk1, lq2, lk2, lam_init):
    return (jnp.exp(jnp.sum(lq1[...] * lk1[...], axis=-1, keepdims=True))
            - jnp.exp(jnp.sum(lq2[...] * lk2[...], axis=-1, keepdims=True)) + lam_init)


def _inproj_kernel(x_ref, g_ref, w_ref, cos_ref, sin_ref, pw_ref, ps_ref,
                   q_ref, k_ref, vt_ref, kf_ref, vf_ref, py_ref, tail_ref, ubuf):
    i = pl.program_id(1)
    tm = x_ref.shape[1]

    @pl.when(i == 0)
    def _():
        ubuf[0:POOL_CARRY, :] = jnp.zeros((POOL_CARRY, POOL_W), F32)

    h = _rms(x_ref[0], g_ref[...]).astype(BF16)
    cos = cos_ref[...]
    sin = sin_ref[...]

    u = jnp.dot(h, w_ref[:, 0:POOL_W], preferred_element_type=F32)
    ubuf[POOL_CARRY:POOL_CARRY + tm, :] = u

    q = jnp.dot(h, w_ref[:, POOL_W:POOL_W + ATTN_W], preferred_element_type=F32)
    k = jnp.dot(h, w_ref[:, POOL_W + ATTN_W:POOL_W + 2 * ATTN_W], preferred_element_type=F32)
    v = jnp.dot(h, w_ref[:, POOL_W + 2 * ATTN_W:IN_W], preferred_element_type=F32)
    for hd in range(N_HEADS):
        sl = slice(hd * VDIM, (hd + 1) * VDIM)
        qh = _rope(q[:, sl], cos, sin)
        q_ref[0, :, sl] = (qh * Q_SCALE).astype(BF16)
        kh = _rope(k[:, sl], cos, sin)
        kf_ref[0, :, sl] = kh
        k_ref[0, :, sl] = kh.astype(BF16)
        vh = v[:, sl]
        vf_ref[0, :, sl] = vh
        vt_ref[0, hd, 0] = vh.T.astype(BF16)

    pos = i * tm + lax.broadcasted_iota(jnp.int32, (tm, 1), 0)
    for g, w in enumerate(POOL_WINDOWS):
        sl = slice(g * POOL_GC, (g + 1) * POOL_GC)
        ug = u[:, sl]
        acc = ug
        for d in range(1, w):
            acc = acc + ubuf[POOL_CARRY - d:POOL_CARRY - d + tm, sl]
        cnt = jnp.minimum(pos + 1, w).astype(F32)
        dlt = (acc / cnt - ug).astype(BF16)
        y = jnp.dot(dlt, pw_ref[g], preferred_element_type=F32) * ps_ref[:, sl]
        py_ref[0, :, sl] = y.astype(BF16)

    tail = ubuf[tm:tm + POOL_CARRY, :]
    tail_ref[0] = tail
    ubuf[0:POOL_CARRY, :] = tail


def _inproj(x, g1, w_in, cos, sin, pool_w, pool_scale):
    B, S, _ = x.shape
    tm = ROW_TILE
    nb = S // tm
    row = lambda b, i: (b, i, 0)
    const2 = lambda b, i: (0, 0)
    return pl.pallas_call(
        _inproj_kernel,
        grid=(B, nb),
        in_specs=[
            pl.BlockSpec((1, tm, D_MODEL), row),
            pl.BlockSpec((1, D_MODEL), const2),
            pl.BlockSpec((D_MODEL, IN_W), const2),
            pl.BlockSpec((tm, VDIM), lambda b, i: (i, 0)),
            pl.BlockSpec((tm, VDIM), lambda b, i: (i, 0)),
            pl.BlockSpec((len(POOL_WINDOWS), POOL_GC, POOL_GC), lambda b, i: (0, 0, 0)),
            pl.BlockSpec((1, POOL_W), const2),
        ],
        out_specs=[
            pl.BlockSpec((1, tm, ATTN_W), row),
            pl.BlockSpec((1, tm, ATTN_W), row),
            pl.BlockSpec((1, N_HEADS, 1, VDIM, tm), lambda b, i: (b, 0, i, 0, 0)),
            pl.BlockSpec((1, tm, ATTN_W), row),
            pl.BlockSpec((1, tm, ATTN_W), row),
            pl.BlockSpec((1, tm, POOL_W), row),
            pl.BlockSpec((1, POOL_CARRY, POOL_W), lambda b, i: (b, 0, 0)),
        ],
        out_shape=[
            jax.ShapeDtypeStruct((B, S, ATTN_W), BF16),
            jax.ShapeDtypeStruct((B, S, ATTN_W), BF16),
            jax.ShapeDtypeStruct((B, N_HEADS, nb, VDIM, tm), BF16),
            jax.ShapeDtypeStruct((B, S, ATTN_W), F32),
            jax.ShapeDtypeStruct((B, S, ATTN_W), F32),
            jax.ShapeDtypeStruct((B, S, POOL_W), BF16),
            jax.ShapeDtypeStruct((B, POOL_CARRY, POOL_W), F32),
        ],
        scratch_shapes=[pltpu.VMEM((POOL_CARRY + tm, POOL_W), F32)],
        compiler_params=pltpu.CompilerParams(
            dimension_semantics=("arbitrary", "arbitrary"), vmem_limit_bytes=VMEM_LIMIT),
        name="prompt_inproj",
    )(x, g1, w_in, cos, sin, pool_w, pool_scale)


def _attn_kernel(lq1, lk1, lq2, lk2, g_ref, q_ref, k_ref, vt_ref, o_ref,
                 m_ref, l_ref, acc_ref, *, lam_init):
    qi = pl.program_id(2)
    t = q_ref.shape[1]
    q = q_ref[0].astype(F32)
    lane = lax.broadcasted_iota(jnp.int32, q.shape, 1)
    qz = (jnp.where(lane < HEAD_DIM, q, 0.0).astype(BF16),
          jnp.where(lane >= HEAD_DIM, q, 0.0).astype(BF16))

    m_ref[...] = jnp.full(m_ref.shape, NEG_BIG, F32)
    l_ref[...] = jnp.zeros(l_ref.shape, F32)
    acc_ref[...] = jnp.zeros(acc_ref.shape, F32)

    def block(j, masked):
        ks = pl.multiple_of(j * t, t)
        kb = k_ref[0, pl.ds(ks, t), :]
        vtb = vt_ref[0, 0, j]
        for c in range(2):
            s = lax.dot_general(kb, qz[c], (((1,), (1,)), ((), ())),
                                preferred_element_type=F32)
            if masked:
                key = lax.broadcasted_iota(jnp.int32, s.shape, 0)
                qry = lax.broadcasted_iota(jnp.int32, s.shape, 1)
                s = jnp.where(key <= qry, s, NEG_BIG)
            m_old = m_ref[c]
            m_new = jnp.maximum(m_old, jnp.max(s, axis=0, keepdims=True))
            alpha = jnp.exp(m_old - m_new)
            p = jnp.exp(s - m_new)
            l_ref[c] = alpha * l_ref[c] + jnp.sum(p, axis=0, keepdims=True)
            pv = jnp.dot(vtb, p.astype(BF16), preferred_element_type=F32)
            acc_ref[c] = acc_ref[c] * alpha + pv
            m_ref[c] = m_new

    def body(j, carry):
        block(j, False)
        return carry

    lax.fori_loop(0, qi, body, 0)
    block(qi, True)

    lam = _lam(lq1, lk1, lq2, lk2, lam_init)
    o = acc_ref[0] / l_ref[0] - lam * (acc_ref[1] / l_ref[1])
    y = o * lax.rsqrt(jnp.mean(o * o, axis=0, keepdims=True) + EPS) * g_ref[...]
    o_ref[0] = (y * (1.0 - lam_init)).T.astype(BF16)


def _attention(q, k, vt, lams, subln_col, lam_init):
    B, S, _ = q.shape
    t = ROW_TILE
    nb = S // t
    vec = pl.BlockSpec((1, HEAD_DIM), lambda b, h, i: (0, 0))
    return pl.pallas_call(
        functools.partial(_attn_kernel, lam_init=lam_init),
        grid=(B, N_HEADS, nb),
        in_specs=[
            vec, vec, vec, vec,
            pl.BlockSpec((VDIM, 1), lambda b, h, i: (0, 0)),
            pl.BlockSpec((1, t, VDIM), lambda b, h, i: (b, i, h)),
            pl.BlockSpec((1, S, VDIM), lambda b, h, i: (b, 0, h)),
            pl.BlockSpec((1, 1, nb, VDIM, t), lambda b, h, i: (b, h, 0, 0, 0)),
        ],
        out_specs=pl.BlockSpec((1, t, VDIM), lambda b, h, i: (b, i, h)),
        out_shape=jax.ShapeDtypeStruct((B, S, ATTN_W), BF16),
        scratch_shapes=[
            pltpu.VMEM((2, 1, t), F32),
            pltpu.VMEM((2, 1, t), F32),
            pltpu.VMEM((2, VDIM, t), F32),
        ],
        compiler_params=pltpu.CompilerParams(
            dimension_semantics=("arbitrary", "arbitrary", "arbitrary"),
            vmem_limit_bytes=VMEM_LIMIT),
        name="prompt_attention",
    )(*lams, subln_col, q, k, vt)


def _ffn_chunks(h2_ref, wa_ref, wb_ref, cw_ref, cb_ref, wd_ref, acc_ref, conv_rows):
    def chunk(j, carry):
        h2 = h2_ref[...]
        a = jnp.dot(h2, wa_ref[j], preferred_element_type=F32)
        bg = jnp.dot(h2, wb_ref[j], preferred_element_type=F32)
        a2, a1 = conv_rows(j, a)
        cw = cw_ref[j]
        c = cw[0:1] * a2 + cw[1:2] * a1 + cw[2:3] * a + cb_ref[j]
        gated = (c * jax.nn.sigmoid(c)) * bg
        acc_ref[...] += jnp.dot(gated.astype(BF16), wd_ref[j], preferred_element_type=F32)
        return carry

    lax.fori_loop(0, N_FF_CHUNKS, chunk, 0)


def _ffn_kernel(x_ref, py_ref, ay_ref, wo_ref, g2_ref, wa_ref, wb_ref, cw_ref, cb_ref, wd_ref,
                fg_ref, out_ref, ctail_ref, carry, abuf, h2_ref, acc_ref, *, final):
    i = pl.program_id(1)
    tm = x_ref.shape[1]

    @pl.when(i == 0)
    def _():
        carry[...] = jnp.zeros(carry.shape, F32)

    x1 = (x_ref[0]
          + jnp.dot(py_ref[0], wo_ref[0:POOL_W, :], preferred_element_type=F32)
          + jnp.dot(ay_ref[0], wo_ref[POOL_W:POOL_W + ATTN_W, :], preferred_element_type=F32))
    h2_ref[...] = _rms(x1, g2_ref[...]).astype(BF16)
    acc_ref[...] = x1

    def conv_rows(j, a):
        abuf[0:CARRY_ROWS, :] = carry[j]
        abuf[CARRY_ROWS:CARRY_ROWS + tm, :] = a
        carry[j] = abuf[tm:tm + CARRY_ROWS, :]
        return (abuf[CARRY_ROWS - 2:CARRY_ROWS - 2 + tm, :],
                abuf[CARRY_ROWS - 1:CARRY_ROWS - 1 + tm, :])

    _ffn_chunks(h2_ref, wa_ref, wb_ref, cw_ref, cb_ref, wd_ref, acc_ref, conv_rows)

    out = acc_ref[...]
    if final:
        out = _rms(out, fg_ref[...])
    out_ref[0] = out
    ctail_ref[0] = carry[...]


def _ffn(x, py, ay, w_out, g2, wa, wb, cw, cb, wd, fg, final):
    B, S, _ = x.shape
    tm = ROW_TILE
    nb = S // tm
    row = lambda b, i: (b, i, 0)
    const2 = lambda b, i: (0, 0)
    const3 = lambda b, i: (0, 0, 0)
    resident = dict(pipeline_mode=pl.Buffered(1))
    return pl.pallas_call(
        functools.partial(_ffn_kernel, final=final),
        grid=(B, nb),
        in_specs=[
            pl.BlockSpec((1, tm, D_MODEL), row),
            pl.BlockSpec((1, tm, POOL_W), row),
            pl.BlockSpec((1, tm, ATTN_W), row),
            pl.BlockSpec((D_MODEL, D_MODEL), const2, **resident),
            pl.BlockSpec((1, D_MODEL), const2),
            pl.BlockSpec((N_FF_CHUNKS, D_MODEL, FF_CHUNK), const3, **resident),
            pl.BlockSpec((N_FF_CHUNKS, D_MODEL, FF_CHUNK), const3, **resident),
            pl.BlockSpec((N_FF_CHUNKS, CONV_W, FF_CHUNK), const3),
            pl.BlockSpec((N_FF_CHUNKS, 1, FF_CHUNK), const3),
            pl.BlockSpec((N_FF_CHUNKS, FF_CHUNK, D_MODEL), const3, **resident),
            pl.BlockSpec((1, D_MODEL), const2),
        ],
        out_specs=[
            pl.BlockSpec((1, tm, D_MODEL), row),
            pl.BlockSpec((1, N_FF_CHUNKS, CARRY_ROWS, FF_CHUNK), lambda b, i: (b, 0, 0, 0)),
        ],
        out_shape=[
            jax.ShapeDtypeStruct((B, S, D_MODEL), F32),
            jax.ShapeDtypeStruct((B, N_FF_CHUNKS, CARRY_ROWS, FF_CHUNK), F32),
        ],
        scratch_shapes=[
            pltpu.VMEM((N_FF_CHUNKS, CARRY_ROWS, FF_CHUNK), F32),
            pltpu.VMEM((CARRY_ROWS + tm, FF_CHUNK), F32),
            pltpu.VMEM((tm, D_MODEL), BF16),
            pltpu.VMEM((tm, D_MODEL), F32),
        ],
        compiler_params=pltpu.CompilerParams(
            dimension_semantics=("arbitrary", "arbitrary"), vmem_limit_bytes=VMEM_LIMIT),
        name="prompt_ffn",
    )(x, py, ay, w_out, g2, wa, wb, cw, cb, wd, fg)


def _s_inproj_kernel(x_ref, g_ref, w_ref, cos_ref, sin_ref, st_ref, pw_ref, ps_ref,
                     q_ref, k_ref, v_ref, u_ref, py_ref, *, pos):
    h = _rms(x_ref[...], g_ref[...]).astype(BF16)
    proj = jnp.dot(h, w_ref[...], preferred_element_type=F32)
    u = proj[:, 0:POOL_W]
    u_ref[...] = u
    cos = cos_ref[...]
    sin = sin_ref[...]
    for hd in range(N_HEADS):
        sl = slice(hd * VDIM, (hd + 1) * VDIM)
        q_ref[:, sl] = _rope(proj[:, POOL_W + hd * VDIM:POOL_W + (hd + 1) * VDIM], cos, sin)
        k_ref[:, sl] = _rope(
            proj[:, POOL_W + ATTN_W + hd * VDIM:POOL_W + ATTN_W + (hd + 1) * VDIM], cos, sin)
    v_ref[...] = proj[:, POOL_W + 2 * ATTN_W:IN_W]
    for g, w in enumerate(POOL_WINDOWS):
        sl = slice(g * POOL_GC, (g + 1) * POOL_GC)
        ug = u[:, sl]
        acc = ug
        for d in range(1, w):
            acc = acc + st_ref[POOL_HIST - d, :, sl]
        cnt = float(min(pos + 1, w))
        dlt = (acc / cnt - ug).astype(BF16)
        y = jnp.dot(dlt, pw_ref[g], preferred_element_type=F32) * ps_ref[:, sl]
        py_ref[:, sl] = y.astype(BF16)


def _s_inproj(x, g1, w_in, cos, sin, state_t, pool_w, pool_scale, pos):
    db = x.shape[0]
    return pl.pallas_call(
        functools.partial(_s_inproj_kernel, pos=pos),
        out_shape=[
            jax.ShapeDtypeStruct((db, ATTN_W), F32),
            jax.ShapeDtypeStruct((db, ATTN_W), F32),
            jax.ShapeDtypeStruct((db, ATTN_W), F32),
            jax.ShapeDtypeStruct((db, POOL_W), F32),
            jax.ShapeDtypeStruct((db, POOL_W), BF16),
        ],
        compiler_params=pltpu.CompilerParams(vmem_limit_bytes=VMEM_LIMIT),
        name="sample_inproj",
    )(x, g1, w_in, cos, sin, state_t, pool_w, pool_scale)


def _s_attn_kernel(pt_ref, lq1, lk1, lq2, lk2, g_ref, q_ref, kn_ref, vn_ref, *rest, lam_init):
    np_ = PAGES_PER_STEP
    k_refs = rest[:np_]
    v_refs = rest[np_:2 * np_]
    o_ref, m_ref, l_ref, acc_ref = rest[2 * np_:]
    j = pl.program_id(1)
    rows = 2 * N_HEADS * 2

    @pl.when(j == 0)
    def _():
        m_ref[...] = jnp.full(m_ref.shape, NEG_BIG, F32)
        l_ref[...] = jnp.zeros(l_ref.shape, F32)
        acc_ref[...] = jnp.zeros(acc_ref.shape, F32)

    r = lax.broadcasted_iota(jnp.int32, (rows, ATTN_W), 0)
    lane = lax.broadcasted_iota(jnp.int32, (rows, ATTN_W), 1)
    qrow = q_ref[0] * Q_SCALE
    qm = jnp.where((lane // HEAD_DIM) == r, jnp.broadcast_to(qrow, (rows, ATTN_W)),
                   0.0).astype(BF16)

    s = jnp.concatenate(
        [lax.dot_general(qm, kr[...].astype(BF16), (((1,), (1,)), ((), ())),
                         preferred_element_type=F32) for kr in k_refs], axis=1)
    m_old = m_ref[...]
    m_new = jnp.maximum(m_old, jnp.max(s, axis=1, keepdims=True))
    alpha = jnp.exp(m_old - m_new)
    p = jnp.exp(s - m_new).astype(BF16)
    l_ref[...] = alpha * l_ref[...] + jnp.sum(p.astype(F32), axis=1, keepdims=True)
    page = k_refs[0].shape[0]
    pv = jnp.zeros((rows, ATTN_W), F32)
    for idx, vr in enumerate(v_refs):
        pv = pv + jnp.dot(p[:, idx * page:(idx + 1) * page], vr[...].astype(BF16),
                          preferred_element_type=F32)
    acc_ref[...] = alpha * acc_ref[...] + pv
    m_ref[...] = m_new

    @pl.when(j == pl.num_programs(1) - 1)
    def _():
        kn = kn_ref[0].astype(BF16).astype(F32)
        vn = vn_ref[0].astype(BF16).astype(F32)
        s_new = jnp.sum(qm.astype(F32) * kn, axis=1, keepdims=True)
        m_o = m_ref[...]
        m_n = jnp.maximum(m_o, s_new)
        a_n = jnp.exp(m_o - m_n)
        p_n = jnp.exp(s_new - m_n)
        l_n = a_n * l_ref[...] + p_n
        acc = a_n * acc_ref[...] + p_n * vn
        lam = _lam(lq1, lk1, lq2, lk2, lam_init)
        coef = jnp.where((r % 2) == 0, 1.0, -lam)
        keep = (lane // VDIM) == (r // 2)
        o = jnp.sum(jnp.where(keep, (acc / l_n) * coef, 0.0), axis=0, keepdims=True)
        g = g_ref[...]
        for hd in range(N_HEADS):
            sl = slice(hd * VDIM, (hd + 1) * VDIM)
            oh = o[:, sl]
            yh = oh * lax.rsqrt(jnp.mean(oh * oh, axis=-1, keepdims=True) + EPS) * g
            o_ref[0, :, sl] = (yh * (1.0 - lam_init)).astype(BF16)


def _s_attention(page_table, lams, subln_row, q, kn, vn, cache_k, cache_v, layer, lam_init):
    db, n_pages = page_table.shape
    np_ = PAGES_PER_STEP
    page = cache_k.shape[2]
    rows = 2 * N_HEADS * 2
    vec = pl.BlockSpec((1, HEAD_DIM), lambda b, j, pt: (0, 0))
    tok = pl.BlockSpec((1, 1, ATTN_W), lambda b, j, pt: (b, 0, 0))

    def page_spec(idx):
        return pl.BlockSpec((None, None, page, ATTN_W),
                            lambda b, j, pt: (layer, pt[b, j * np_ + idx], 0, 0))

    grid_spec = pltpu.PrefetchScalarGridSpec(
        num_scalar_prefetch=1,
        grid=(db, n_pages // np_),
        in_specs=[vec, vec, vec, vec,
                  pl.BlockSpec((1, VDIM), lambda b, j, pt: (0, 0)),
                  tok, tok, tok]
                 + [page_spec(idx) for idx in range(np_)]
                 + [page_spec(idx) for idx in range(np_)],
        out_specs=tok,
        scratch_shapes=[
            pltpu.VMEM((rows, 1), F32),
            pltpu.VMEM((rows, 1), F32),
            pltpu.VMEM((rows, ATTN_W), F32),
        ],
    )
    return pl.pallas_call(
        functools.partial(_s_attn_kernel, lam_init=lam_init),
        grid_spec=grid_spec,
        out_shape=jax.ShapeDtypeStruct((db, 1, ATTN_W), BF16),
        compiler_params=pltpu.CompilerParams(
            dimension_semantics=("arbitrary", "arbitrary"), vmem_limit_bytes=VMEM_LIMIT),
        name="sample_attention",
    )(page_table, *lams, subln_row, q, kn, vn, *([cache_k] * np_), *([cache_v] * np_))


def _s_ffn_kernel(x_ref, py_ref, ay_ref, wo_ref, g2_ref, wa_ref, wb_ref, cw_ref, cb_ref, wd_ref,
                  fg_ref, prev_ref, out_ref, a_ref, h2_ref, acc_ref, *, final):
    x1 = (x_ref[...]
          + jnp.dot(py_ref[...], wo_ref[0:POOL_W, :], preferred_element_type=F32)
          + jnp.dot(ay_ref[...], wo_ref[POOL_W:POOL_W + ATTN_W, :], preferred_element_type=F32))
    h2_ref[...] = _rms(x1, g2_ref[...]).astype(BF16)
    acc_ref[...] = x1

    def conv_rows(j, a):
        a_ref[j] = a
        return prev_ref[j, 0], prev_ref[j, 1]

    _ffn_chunks(h2_ref, wa_ref, wb_ref, cw_ref, cb_ref, wd_ref, acc_ref, conv_rows)

    out = acc_ref[...]
    if final:
        out = _rms(out, fg_ref[...])
    out_ref[...] = out


def _s_ffn(x, py, ay, w_out, g2, wa, wb, cw, cb, wd, fg, prev, final):
    db = x.shape[0]
    return pl.pallas_call(
        functools.partial(_s_ffn_kernel, final=final),
        out_shape=[
            jax.ShapeDtypeStruct((db, D_MODEL), F32),
            jax.ShapeDtypeStruct((N_FF_CHUNKS, db, FF_CHUNK), F32),
        ],
        scratch_shapes=[
            pltpu.VMEM((db, D_MODEL), BF16),
            pltpu.VMEM((db, D_MODEL), F32),
        ],
        compiler_params=pltpu.CompilerParams(vmem_limit_bytes=VMEM_LIMIT),
        name="sample_ffn",
    )(x, py, ay, w_out, g2, wa, wb, cw, cb, wd, fg, prev)


def _rope_tables(pos):
    half = HEAD_DIM // 2
    inv = 1.0 / (ROPE_THETA ** (jnp.arange(half, dtype=F32) * (2.0 / HEAD_DIM)))
    ang = pos.astype(F32)[:, None] * inv[None, :]
    cos = jnp.cos(ang)
    sin = jnp.sin(ang)
    return (jnp.concatenate([cos, cos, cos, cos], axis=1),
            jnp.concatenate([-sin, sin, -sin, sin], axis=1))


def _chunk_cols(w):
    return w.reshape(w.shape[0], N_FF_CHUNKS, FF_CHUNK).transpose(1, 0, 2)


def kernel(x_prompt, x_sample, cache_k, cache_v, state_pool, state_conv, page_table, norm1_g, w_in, pool_w, pool_scale, lam_q1, lam_k1, lam_q2, lam_k2, subln_g, w_out, norm2_g, w_up, conv_w, conv_b, w_down, final_g):
    B, S, _ = x_prompt.shape
    DB, T, _ = x_sample.shape
    depth = w_in.shape[0]
    n_pool, page = cache_k.shape[1], cache_k.shape[2]
    past = page_table.shape[1] * page
    assert T == 1 and S % ROW_TILE == 0 and page_table.shape[1] % PAGES_PER_STEP == 0

    cos_p, sin_p = _rope_tables(jnp.arange(S, dtype=jnp.int32))
    cos_s, sin_s = _rope_tables(past + jnp.arange(T, dtype=jnp.int32))
    ck = cache_k.reshape(depth, n_pool, page, ATTN_W)
    cv = cache_v.reshape(depth, n_pool, page, ATTN_W)
    fg = final_g.reshape(1, D_MODEL)

    xp = x_prompt
    xs = x_sample.reshape(DB, D_MODEL)
    outs = [[] for _ in range(8)]
    for l in range(depth):
        li = _lambda_init(l)
        final = l == depth - 1
        g1 = norm1_g[l].reshape(1, D_MODEL)
        g2 = norm2_g[l].reshape(1, D_MODEL)
        w_in_b = w_in[l].astype(BF16)
        pool_w_b = pool_w[l].astype(BF16)
        ps = pool_scale[l].reshape(1, POOL_W)
        lams = [v[l].reshape(1, HEAD_DIM) for v in (lam_q1, lam_k1, lam_q2, lam_k2)]
        w_out_b = w_out[l].astype(BF16)
        wa = _chunk_cols(w_up[l][:, :D_FF]).astype(BF16)
        wb = _chunk_cols(w_up[l][:, D_FF:]).astype(BF16)
        cw = _chunk_cols(conv_w[l])
        cb = _chunk_cols(conv_b[l].reshape(1, D_FF))
        wd = w_down[l].astype(BF16).reshape(N_FF_CHUNKS, FF_CHUNK, D_MODEL)

        q, k, vt, kf, vf, py, tail = _inproj(xp, g1, w_in_b, cos_p, sin_p, pool_w_b, ps)
        ay = _attention(q, k, vt, lams, subln_g[l].reshape(VDIM, 1), li)
        xp, ctail = _ffn(xp, py, ay, w_out_b, g2, wa, wb, cw, cb, wd, fg, final)
        outs[0].append(kf.reshape(B, S // page, page, N_HEADS, VDIM))
        outs[1].append(vf.reshape(B, S // page, page, N_HEADS, VDIM))
        outs[2].append(tail[:, POOL_CARRY - POOL_HIST:])
        outs[3].append(ctail[:, :, CARRY_ROWS - (CONV_W - 1):, :].transpose(0, 2, 1, 3)
                       .reshape(B, CONV_W - 1, D_FF))

        state_t = state_pool[l].transpose(1, 0, 2)
        qs, ksn, vsn, us, pys = _s_inproj(xs, g1, w_in_b, cos_s, sin_s, state_t, pool_w_b, ps, past)
        ays = _s_attention(page_table, lams, subln_g[l].reshape(1, VDIM),
                           qs.reshape(DB, 1, ATTN_W), ksn.reshape(DB, 1, ATTN_W),
                           vsn.reshape(DB, 1, ATTN_W), ck, cv, l, li)
        prev = _chunk_cols(state_conv[l].reshape(DB * (CONV_W - 1), D_FF)).reshape(
            N_FF_CHUNKS, DB, CONV_W - 1, FF_CHUNK).transpose(0, 2, 1, 3)
        xs, a_s = _s_ffn(xs, pys, ays.reshape(DB, ATTN_W), w_out_b, g2, wa, wb, cw, cb, wd, fg,
                         prev, final)
        a_rows = a_s.transpose(1, 0, 2).reshape(DB, 1, D_FF)
        outs[4].append(ksn.reshape(DB, T, N_HEADS, VDIM))
        outs[5].append(vsn.reshape(DB, T, N_HEADS, VDIM))
        outs[6].append(jnp.concatenate([state_pool[l][:, 1:], us[:, None, :]], axis=1))
        outs[7].append(jnp.concatenate([state_conv[l][:, 1:], a_rows], axis=1))

    y_prompt = xp
    y_sample = xs.reshape(DB, T, D_MODEL)
    stk = [jnp.stack(o) for o in outs]
    return (y_prompt, y_sample, stk[0], stk[1], stk[2], stk[3], stk[4], stk[5], stk[6], stk[7])
```

```python
import functools
import math

import jax
import jax.numpy as jnp
from jax import lax
from jax.experimental import pallas as pl
from jax.experimental.pallas import tpu as pltpu

F32 = jnp.float32
BF16 = jnp.bfloat16

D_MODEL = 1024
POOL_W = 512
POOL_WINDOWS = (2, 4, 8, 16)
POOL_GC = 128
POOL_HIST = 15
ATTN_W = 512
HEAD_DIM = 64
VDIM = 128
N_HEADS = 4
IN_W = POOL_W + 3 * ATTN_W
D_FF = 2816
CONV_W = 3
ROPE_THETA = 10000.0
EPS = 1e-6
Q_SCALE = HEAD_DIM ** -0.5 * math.log2(math.e)
NEG_BIG = -1e30

ROW_TILE = 512
FF_CHUNK = 256
N_FF_CHUNKS = D_FF // FF_CHUNK
CARRY_ROWS = 8
POOL_CARRY = 16
HEADS_PER_STEP = 2
PAGES_PER_STEP = 16
SCORE_ROWS = 16
VMEM_LIMIT = 56 * 1024 * 1024


def _lambda_init(layer):
    return 0.8 - 0.6 * math.exp(-0.3 * layer)


def _rms(x, g):
    return x * lax.rsqrt(jnp.mean(x * x, axis=-1, keepdims=True) + EPS) * g


def _rope(x, cos, sin_signed):
    lane = lax.broadcasted_iota(jnp.int32, x.shape, 1)
    rot = jnp.where((lane % HEAD_DIM) < HEAD_DIM // 2,
                    pltpu.roll(x, VDIM - HEAD_DIM // 2, 1), pltpu.roll(x, HEAD_DIM // 2, 1))
    return x * cos + rot * sin_signed


def _lam(lq1, lk1, lq2, lk2, lam_init):
    return (jnp.exp(jnp.sum(lq1[...] * lk1[...], axis=-1, keepdims=True))
            - jnp.exp(jnp.sum(lq2[...] * lk2[...], axis=-1, keepdims=True)) + lam_init)


def _inproj_kernel(x_ref, g_ref, w_ref, cos_ref, sin_ref, pw_ref, ps_ref, *rest):
    q_ref, k_ref, vt_ref, kf_ref, vf_ref, py_ref, tail_ref, ubuf = rest[-8:]
    i = pl.program_id(1)
    tm = x_ref.shape[1]

    @pl.when(i == 0)
    def _():
        ubuf[0:POOL_CARRY, :] = jnp.zeros((POOL_CARRY, POOL_W), F32)

    h = _rms(x_ref[0], g_ref[...]).astype(BF16)
    cos = cos_ref[...]
    sin = sin_ref[...]

    u = jnp.dot(h, w_ref[:, 0:POOL_W], preferred_element_type=F32)
    ubuf[POOL_CARRY:POOL_CARRY + tm, :] = u

    q = jnp.dot(h, w_ref[:, POOL_W:POOL_W + ATTN_W], preferred_element_type=F32)
    k = jnp.dot(h, w_ref[:, POOL_W + ATTN_W:POOL_W + 2 * ATTN_W], preferred_element_type=F32)
    v = jnp.dot(h, w_ref[:, POOL_W + 2 * ATTN_W:IN_W], preferred_element_type=F32)
    for hd in range(N_HEADS):
        sl = slice(hd * VDIM, (hd + 1) * VDIM)
        qh = _rope(q[:, sl], cos, sin)
        q_ref[0, :, sl] = (qh * Q_SCALE).astype(BF16)
        kh = _rope(k[:, sl], cos, sin)
        k_ref[0, :, sl] = kh.astype(BF16)
        vh = v[:, sl]
        vt_ref[0, hd, 0] = vh.T.astype(BF16)
        kf_ref[pl.ds(hd, tm, stride=N_HEADS), :] = kh
        vf_ref[pl.ds(hd, tm, stride=N_HEADS), :] = vh

    pos = i * tm + lax.broadcasted_iota(jnp.int32, (tm, 1), 0)
    for g, w in enumerate(POOL_WINDOWS):
        sl = slice(g * POOL_GC, (g + 1) * POOL_GC)
        ug = u[:, sl]
        acc = ug
        for d in range(1, w):
            acc = acc + ubuf[POOL_CARRY - d:POOL_CARRY - d + tm, sl]
        cnt = jnp.minimum(pos + 1, w).astype(F32)
        dlt = (acc / cnt - ug).astype(BF16)
        y = jnp.dot(dlt, pw_ref[g], preferred_element_type=F32) * ps_ref[:, sl]
        py_ref[0, :, sl] = y.astype(BF16)

    tail = ubuf[tm:tm + POOL_CARRY, :]
    tail_ref[0] = tail
    ubuf[0:POOL_CARRY, :] = tail


def _inproj(x, g1, w_in, cos, sin, pool_w, pool_scale, layer, depth, cache_rows):
    B, S, _ = x.shape
    tm = ROW_TILE
    nb = S // tm
    row = lambda b, i: (b, i, 0)
    const2 = lambda b, i: (0, 0)
    cache_spec = pl.BlockSpec((None, None, N_HEADS * tm, VDIM), lambda b, i: (layer, b, i, 0))
    cache_shape = jax.ShapeDtypeStruct((depth, B, N_HEADS * S, VDIM), F32)
    in_specs = [
        pl.BlockSpec((1, tm, D_MODEL), row),
        pl.BlockSpec((1, D_MODEL), const2),
        pl.BlockSpec((D_MODEL, IN_W), const2),
        pl.BlockSpec((tm, VDIM), lambda b, i: (i, 0)),
        pl.BlockSpec((tm, VDIM), lambda b, i: (i, 0)),
        pl.BlockSpec((len(POOL_WINDOWS), POOL_GC, POOL_GC), lambda b, i: (0, 0, 0)),
        pl.BlockSpec((1, POOL_W), const2),
    ]
    args = [x, g1, w_in, cos, sin, pool_w, pool_scale]
    aliases = {}
    if cache_rows is not None:
        in_specs += [pl.BlockSpec(memory_space=pl.ANY)] * 2
        aliases = {len(args): 3, len(args) + 1: 4}
        args += list(cache_rows)
    return pl.pallas_call(
        _inproj_kernel,
        grid=(B, nb),
        in_specs=in_specs,
        out_specs=[
            pl.BlockSpec((1, tm, ATTN_W), row),
            pl.BlockSpec((1, tm, ATTN_W), row),
            pl.BlockSpec((1, N_HEADS, 1, VDIM, tm), lambda b, i: (b, 0, i, 0, 0)),
            cache_spec,
            cache_spec,
            pl.BlockSpec((1, tm, POOL_W), row),
            pl.BlockSpec((1, POOL_CARRY, POOL_W), lambda b, i: (b, 0, 0)),
        ],
        out_shape=[
            jax.ShapeDtypeStruct((B, S, ATTN_W), BF16),
            jax.ShapeDtypeStruct((B, S, ATTN_W), BF16),
            jax.ShapeDtypeStruct((B, N_HEADS, nb, VDIM, tm), BF16),
            cache_shape,
            cache_shape,
            jax.ShapeDtypeStruct((B, S, POOL_W), BF16),
            jax.ShapeDtypeStruct((B, POOL_CARRY, POOL_W), F32),
        ],
        input_output_aliases=aliases,
        scratch_shapes=[pltpu.VMEM((POOL_CARRY + tm, POOL_W), F32)],
        compiler_params=pltpu.CompilerParams(
            dimension_semantics=("arbitrary", "arbitrary"), vmem_limit_bytes=VMEM_LIMIT),
        name="prompt_inproj",
    )(*args)


def _attn_kernel(lq1, lk1, lq2, lk2, g_ref, q_ref, k_ref, vt_ref, o_ref,
                 m_ref, l_ref, acc_ref, *, lam_init):
    qi = pl.program_id(2)
    t = q_ref.shape[1]
    n_chain = 2 * HEADS_PER_STEP
    qz = []
    for hl in range(HEADS_PER_STEP):
        q = q_ref[0, :, hl * VDIM:(hl + 1) * VDIM].astype(F32)
        lane = lax.broadcasted_iota(jnp.int32, q.shape, 1)
        qz.append(jnp.where(lane < HEAD_DIM, q, 0.0).astype(BF16))
        qz.append(jnp.where(lane >= HEAD_DIM, q, 0.0).astype(BF16))

    m_ref[...] = jnp.full(m_ref.shape, NEG_BIG, F32)
    l_ref[...] = jnp.zeros(l_ref.shape, F32)
    acc_ref[...] = jnp.zeros(acc_ref.shape, F32)

    def block(j, masked):
        ks = pl.multiple_of(j * t, t)
        s = []
        for ch in range(n_chain):
            hl = ch // 2
            kb = k_ref[0, pl.ds(ks, t), hl * VDIM:(hl + 1) * VDIM]
            s.append(lax.dot_general(kb, qz[ch], (((1,), (1,)), ((), ())),
                                     preferred_element_type=F32))
        p = []
        alpha = []
        for ch in range(n_chain):
            sc = s[ch]
            if masked:
                key = lax.broadcasted_iota(jnp.int32, sc.shape, 0)
                qry = lax.broadcasted_iota(jnp.int32, sc.shape, 1)
                sc = jnp.where(key <= qry, sc, NEG_BIG)
            m_old = m_ref[ch]
            m_new = jnp.maximum(m_old, jnp.max(sc, axis=0, keepdims=True))
            a = jnp.exp2(m_old - m_new)
            pc = jnp.exp2(sc - m_new)
            l_ref[ch] = a * l_ref[ch] + jnp.sum(pc, axis=0, keepdims=True)
            m_ref[ch] = m_new
            p.append(pc.astype(BF16))
            alpha.append(a)
        for ch in range(n_chain):
            pv = jnp.dot(vt_ref[0, ch // 2, j], p[ch], preferred_element_type=F32)
            acc_ref[ch] = acc_ref[ch] * alpha[ch] + pv

    def body(j, carry):
        block(j, False)
        return carry

    lax.fori_loop(0, qi, body, 0)
    block(qi, True)

    lam = _lam(lq1, lk1, lq2, lk2, lam_init)
    for hl in range(HEADS_PER_STEP):
        o = (acc_ref[2 * hl] / l_ref[2 * hl]
             - lam * (acc_ref[2 * hl + 1] / l_ref[2 * hl + 1]))
        y = o * lax.rsqrt(jnp.mean(o * o, axis=0, keepdims=True) + EPS) * g_ref[...]
        o_ref[0, :, hl * VDIM:(hl + 1) * VDIM] = (y * (1.0 - lam_init)).T.astype(BF16)


def _attention(q, k, vt, lams, subln_col, lam_init):
    B, S, _ = q.shape
    t = ROW_TILE
    nb = S // t
    hs = HEADS_PER_STEP
    vec = pl.BlockSpec((1, HEAD_DIM), lambda b, h, i: (0, 0))
    return pl.pallas_call(
        functools.partial(_attn_kernel, lam_init=lam_init),
        grid=(B, N_HEADS // hs, nb),
        in_specs=[
            vec, vec, vec, vec,
            pl.BlockSpec((VDIM, 1), lambda b, h, i: (0, 0)),
            pl.BlockSpec((1, t, hs * VDIM), lambda b, h, i: (b, i, h)),
            pl.BlockSpec((1, S, hs * VDIM), lambda b, h, i: (b, 0, h)),
            pl.BlockSpec((1, hs, nb, VDIM, t), lambda b, h, i: (b, h, 0, 0, 0)),
        ],
        out_specs=pl.BlockSpec((1, t, hs * VDIM), lambda b, h, i: (b, i, h)),
        out_shape=jax.ShapeDtypeStruct((B, S, ATTN_W), BF16),
        scratch_shapes=[
            pltpu.VMEM((2 * hs, 1, t), F32),
            pltpu.VMEM((2 * hs, 1, t), F32),
            pltpu.VMEM((2 * hs, VDIM, t), F32),
        ],
        compiler_params=pltpu.CompilerParams(
            dimension_semantics=("arbitrary", "arbitrary", "arbitrary"),
            vmem_limit_bytes=VMEM_LIMIT),
        name="prompt_attention",
    )(*lams, subln_col, q, k, vt)


def _ffn_gate(j, a, bg, a2, a1, cw_ref, cb_ref):
    cw = cw_ref[j]
    c = cw[0:1] * a2 + cw[1:2] * a1 + cw[2:3] * a + cb_ref[j]
    return ((c * jax.nn.sigmoid(c)) * bg).astype(BF16)


def _ffn_kernel(x_ref, py_ref, ay_ref, wo_ref, g2_ref, wa_ref, wb_ref, cw_ref, cb_ref, wd_ref,
                fg_ref, out_ref, ctail_ref, carry, abuf, bbuf, h2_ref, acc_ref, gbuf, *, final):
    i = pl.program_id(1)
    tm = x_ref.shape[1]

    @pl.when(i == 0)
    def _():
        carry[...] = jnp.zeros(carry.shape, F32)

    x1 = (x_ref[0]
          + jnp.dot(py_ref[0], wo_ref[0:POOL_W, :], preferred_element_type=F32)
          + jnp.dot(ay_ref[0], wo_ref[POOL_W:POOL_W + ATTN_W, :], preferred_element_type=F32))
    h2_ref[...] = _rms(x1, g2_ref[...]).astype(BF16)

    def up(j):
        h2 = h2_ref[...]
        abuf[j % 2, CARRY_ROWS:CARRY_ROWS + tm, :] = jnp.dot(
            h2, wa_ref[j], preferred_element_type=F32)
        bbuf[j % 2] = jnp.dot(h2, wb_ref[j], preferred_element_type=F32)

    acc_ref[...] = x1
    up(0)
    for j in range(N_FF_CHUNKS):
        if j + 1 < N_FF_CHUNKS:
            up(j + 1)
        ab = abuf.at[j % 2]
        ab[0:CARRY_ROWS, :] = carry[j]
        carry[j] = ab[tm:tm + CARRY_ROWS, :]
        gated = _ffn_gate(j, ab[CARRY_ROWS:CARRY_ROWS + tm, :], bbuf[j % 2],
                          ab[CARRY_ROWS - 2:CARRY_ROWS - 2 + tm, :],
                          ab[CARRY_ROWS - 1:CARRY_ROWS - 1 + tm, :], cw_ref, cb_ref)
        gbuf[:, j * FF_CHUNK:(j + 1) * FF_CHUNK] = gated

    acc = acc_ref[...] + jnp.dot(gbuf[...], wd_ref[...], preferred_element_type=F32)
    if final:
        acc = _rms(acc, fg_ref[...])
    out_ref[0] = acc
    ctail_ref[0] = carry[...]


def _ffn(x, py, ay, w_out, g2, wa, wb, cw, cb, wd, fg, final):
    B, S, _ = x.shape
    tm = ROW_TILE
    nb = S // tm
    row = lambda b, i: (b, i, 0)
    const2 = lambda b, i: (0, 0)
    const3 = lambda b, i: (0, 0, 0)
    resident = dict(pipeline_mode=pl.Buffered(1))
    return pl.pallas_call(
        functools.partial(_ffn_kernel, final=final),
        grid=(B, nb),
        in_specs=[
            pl.BlockSpec((1, tm, D_MODEL), row),
            pl.BlockSpec((1, tm, POOL_W), row),
            pl.BlockSpec((1, tm, ATTN_W), row),
            pl.BlockSpec((D_MODEL, D_MODEL), const2, **resident),
            pl.BlockSpec((1, D_MODEL), const2),
            pl.BlockSpec((N_FF_CHUNKS, D_MODEL, FF_CHUNK), const3, **resident),
            pl.BlockSpec((N_FF_CHUNKS, D_MODEL, FF_CHUNK), const3, **resident),
            pl.BlockSpec((N_FF_CHUNKS, CONV_W, FF_CHUNK), const3),
            pl.BlockSpec((N_FF_CHUNKS, 1, FF_CHUNK), const3),
            pl.BlockSpec((D_FF, D_MODEL), const2, **resident),
            pl.BlockSpec((1, D_MODEL), const2),
        ],
        out_specs=[
            pl.BlockSpec((1, tm, D_MODEL), row),
            pl.BlockSpec((1, N_FF_CHUNKS, CARRY_ROWS, FF_CHUNK), lambda b, i: (b, 0, 0, 0)),
        ],
        out_shape=[
            jax.ShapeDtypeStruct((B, S, D_MODEL), F32),
            jax.ShapeDtypeStruct((B, N_FF_CHUNKS, CARRY_ROWS, FF_CHUNK), F32),
        ],
        scratch_shapes=[
            pltpu.VMEM((N_FF_CHUNKS, CARRY_ROWS, FF_CHUNK), F32),
            pltpu.VMEM((2, CARRY_ROWS + tm, FF_CHUNK), F32),
            pltpu.VMEM((2, tm, FF_CHUNK), F32),
            pltpu.VMEM((tm, D_MODEL), BF16),
            pltpu.VMEM((tm, D_MODEL), F32),
            pltpu.VMEM((tm, D_FF), BF16),
        ],
        compiler_params=pltpu.CompilerParams(
            dimension_semantics=("arbitrary", "arbitrary"), vmem_limit_bytes=VMEM_LIMIT),
        name="prompt_ffn",
    )(x, py, ay, w_out, g2, wa, wb, cw, cb, wd, fg)


def _s_inproj_kernel(x_ref, g_ref, w_ref, cos_ref, sin_ref, st_ref, pw_ref, ps_ref,
                     q_ref, k_ref, v_ref, u_ref, py_ref, *, pos):
    h = _rms(x_ref[...], g_ref[...]).astype(BF16)
    proj = jnp.dot(h, w_ref[...], preferred_element_type=F32)
    u = proj[:, 0:POOL_W]
    u_ref[...] = u
    cos = cos_ref[...]
    sin = sin_ref[...]
    for hd in range(N_HEADS):
        sl = slice(hd * VDIM, (hd + 1) * VDIM)
        q_ref[:, sl] = _rope(proj[:, POOL_W + hd * VDIM:POOL_W + (hd + 1) * VDIM], cos, sin)
        k_ref[:, sl] = _rope(
            proj[:, POOL_W + ATTN_W + hd * VDIM:POOL_W + ATTN_W + (hd + 1) * VDIM], cos, sin)
    v_ref[...] = proj[:, POOL_W + 2 * ATTN_W:IN_W]
    for g, w in enumerate(POOL_WINDOWS):
        sl = slice(g * POOL_GC, (g + 1) * POOL_GC)
        ug = u[:, sl]
        acc = ug
        for d in range(1, w):
            acc = acc + st_ref[POOL_HIST - d, :, sl]
        cnt = float(min(pos + 1, w))
        dlt = (acc / cnt - ug).astype(BF16)
        y = jnp.dot(dlt, pw_ref[g], preferred_element_type=F32) * ps_ref[:, sl]
        py_ref[:, sl] = y.astype(BF16)


def _s_inproj(x, g1, w_in, cos, sin, state_t, pool_w, pool_scale, pos):
    db = x.shape[0]
    return pl.pallas_call(
        functools.partial(_s_inproj_kernel, pos=pos),
        out_shape=[
            jax.ShapeDtypeStruct((db, ATTN_W), F32),
            jax.ShapeDtypeStruct((db, ATTN_W), F32),
            jax.ShapeDtypeStruct((db, ATTN_W), F32),
            jax.ShapeDtypeStruct((db, POOL_W), F32),
            jax.ShapeDtypeStruct((db, POOL_W), BF16),
        ],
        compiler_params=pltpu.CompilerParams(vmem_limit_bytes=VMEM_LIMIT),
        name="sample_inproj",
    )(x, g1, w_in, cos, sin, state_t, pool_w, pool_scale)


def _s_attn_kernel(pt_ref, lq1, lk1, lq2, lk2, g_ref, q_ref, kn_ref, vn_ref, *rest, lam_init):
    np_ = PAGES_PER_STEP
    k_refs = rest[:np_]
    v_refs = rest[np_:2 * np_]
    o_ref, m_ref, l_ref, acc_ref = rest[2 * np_:]
    j = pl.program_id(1)
    rows = SCORE_ROWS
    prow = k_refs[0].shape[0]

    @pl.when(j == 0)
    def _():
        m_ref[...] = jnp.full(m_ref.shape, NEG_BIG, F32)
        l_ref[...] = jnp.zeros(l_ref.shape, F32)
        acc_ref[...] = jnp.zeros(acc_ref.shape, F32)

    r = lax.broadcasted_iota(jnp.int32, (rows, VDIM), 0)
    lane = lax.broadcasted_iota(jnp.int32, (rows, VDIM), 1)
    qm = jnp.where((lane // HEAD_DIM) == (r % 2), q_ref[0] * Q_SCALE, 0.0).astype(BF16)

    rr = lax.broadcasted_iota(jnp.int32, (rows, prow), 0)
    col = lax.broadcasted_iota(jnp.int32, (rows, prow), 1)
    own = (col % N_HEADS) == (rr // 2)
    s = [jnp.where(own, lax.dot_general(qm, kr[...].astype(BF16), (((1,), (1,)), ((), ())),
                                        preferred_element_type=F32), NEG_BIG)
         for kr in k_refs]
    m_blk = s[0]
    for sp in s[1:]:
        m_blk = jnp.maximum(m_blk, sp)
    m_old = m_ref[...]
    m_new = jnp.maximum(m_old, jnp.max(m_blk, axis=1, keepdims=True))
    alpha = jnp.exp2(m_old - m_new)
    p = [jnp.exp2(sp - m_new).astype(BF16) for sp in s]
    l_blk = p[0].astype(F32)
    for pp in p[1:]:
        l_blk = l_blk + pp.astype(F32)
    l_ref[...] = alpha * l_ref[...] + jnp.sum(l_blk, axis=1, keepdims=True)
    pv = jnp.zeros((rows, VDIM), F32)
    for pp, vr in zip(p, v_refs):
        pv = pv + jnp.dot(pp, vr[...].astype(BF16), preferred_element_type=F32)
    acc_ref[...] = alpha * acc_ref[...] + pv
    m_ref[...] = m_new

    @pl.when(j == pl.num_programs(1) - 1)
    def _():
        kn = kn_ref[0].astype(BF16).astype(F32)
        vn = vn_ref[0].astype(BF16).astype(F32)
        s_new = jnp.sum(qm.astype(F32) * kn, axis=1, keepdims=True)
        m_o = m_ref[...]
        m_n = jnp.maximum(m_o, s_new)
        a_n = jnp.exp2(m_o - m_n)
        p_n = jnp.exp2(s_new - m_n)
        l_n = a_n * l_ref[...] + p_n
        on = (a_n * acc_ref[...] + p_n.astype(BF16).astype(F32) * vn) / l_n
        lam = _lam(lq1, lk1, lq2, lk2, lam_init)
        o = jnp.concatenate(
            [on[2 * hd:2 * hd + 1, :] - lam * on[2 * hd + 1:2 * hd + 2, :]
             for hd in range(N_HEADS)], axis=0)
        y = o * lax.rsqrt(jnp.mean(o * o, axis=-1, keepdims=True) + EPS) * g_ref[...]
        o_ref[0] = y * (1.0 - lam_init)


def _s_attention(page_table, lams, subln_row, q16, kn16, vn16, cache_k, cache_v, layer, lam_init):
    db, n_pages = page_table.shape
    np_ = PAGES_PER_STEP
    prow = cache_k.shape[2]
    rows = SCORE_ROWS
    vec = pl.BlockSpec((1, HEAD_DIM), lambda b, j, pt: (0, 0))
    tok = pl.BlockSpec((1, rows, VDIM), lambda b, j, pt: (b, 0, 0))

    def page_spec(idx):
        return pl.BlockSpec((None, None, prow, VDIM),
                            lambda b, j, pt: (layer, pt[b, j * np_ + idx], 0, 0))

    grid_spec = pltpu.PrefetchScalarGridSpec(
        num_scalar_prefetch=1,
        grid=(db, n_pages // np_),
        in_specs=[vec, vec, vec, vec,
                  pl.BlockSpec((1, VDIM), lambda b, j, pt: (0, 0)),
                  tok, tok, tok]
                 + [page_spec(idx) for idx in range(np_)]
                 + [page_spec(idx) for idx in range(np_)],
        out_specs=pl.BlockSpec((1, N_HEADS, VDIM), lambda b, j, pt: (b, 0, 0)),
        scratch_shapes=[
            pltpu.VMEM((rows, 1), F32),
            pltpu.VMEM((rows, 1), F32),
            pltpu.VMEM((rows, VDIM), F32),
        ],
    )
    return pl.pallas_call(
        functools.partial(_s_attn_kernel, lam_init=lam_init),
        grid_spec=grid_spec,
        out_shape=jax.ShapeDtypeStruct((db, N_HEADS, VDIM), F32),
        compiler_params=pltpu.CompilerParams(
            dimension_semantics=("arbitrary", "arbitrary"), vmem_limit_bytes=VMEM_LIMIT),
        name="sample_attention",
    )(page_table, *lams, subln_row, q16, kn16, vn16, *([cache_k] * np_), *([cache_v] * np_))


def _s_ffn_kernel(x_ref, py_ref, ay_ref, wo_ref, g2_ref, wa_ref, wb_ref, cw_ref, cb_ref, wd_ref,
                  fg_ref, prev_ref, out_ref, a_ref, h2_ref, acc_ref, *, final):
    x1 = (x_ref[...]
          + jnp.dot(py_ref[...], wo_ref[0:POOL_W, :], preferred_element_type=F32)
          + jnp.dot(ay_ref[...].astype(BF16), wo_ref[POOL_W:POOL_W + ATTN_W, :],
                    preferred_element_type=F32))
    h2_ref[...] = _rms(x1, g2_ref[...]).astype(BF16)
    acc_ref[...] = x1

    def chunk(j, c):
        h2 = h2_ref[...]
        a = jnp.dot(h2, wa_ref[j], preferred_element_type=F32)
        bg = jnp.dot(h2, wb_ref[j], preferred_element_type=F32)
        a_ref[j] = a
        gated = _ffn_gate(j, a, bg, prev_ref[j, 0], prev_ref[j, 1], cw_ref, cb_ref)
        acc_ref[...] += jnp.dot(gated, wd_ref[j], preferred_element_type=F32)
        return c

    lax.fori_loop(0, N_FF_CHUNKS, chunk, 0)

    out = acc_ref[...]
    if final:
        out = _rms(out, fg_ref[...])
    out_ref[...] = out


def _s_ffn(x, py, ay, w_out, g2, wa, wb, cw, cb, wd, fg, prev, final):
    db = x.shape[0]
    return pl.pallas_call(
        functools.partial(_s_ffn_kernel, final=final),
        out_shape=[
            jax.ShapeDtypeStruct((db, D_MODEL), F32),
            jax.ShapeDtypeStruct((N_FF_CHUNKS, db, FF_CHUNK), F32),
        ],
        scratch_shapes=[
            pltpu.VMEM((db, D_MODEL), BF16),
            pltpu.VMEM((db, D_MODEL), F32),
        ],
        compiler_params=pltpu.CompilerParams(vmem_limit_bytes=VMEM_LIMIT),
        name="sample_ffn",
    )(x, py, ay, w_out, g2, wa, wb, cw, cb, wd, fg, prev)


def _rope_tables(pos):
    half = HEAD_DIM // 2
    inv = 1.0 / (ROPE_THETA ** (jnp.arange(half, dtype=F32) * (2.0 / HEAD_DIM)))
    ang = pos.astype(F32)[:, None] * inv[None, :]
    cos = jnp.cos(ang)
    sin = jnp.sin(ang)
    return (jnp.concatenate([cos, cos, cos, cos], axis=1),
            jnp.concatenate([-sin, sin, -sin, sin], axis=1))


def _chunk_cols(w):
    return w.reshape(w.shape[0], N_FF_CHUNKS, FF_CHUNK).transpose(1, 0, 2)


def _score_rows(x):
    db = x.shape[0]
    x = jnp.repeat(x.reshape(db, N_HEADS, VDIM), 2, axis=1)
    return jnp.pad(x, ((0, 0), (0, SCORE_ROWS - 2 * N_HEADS), (0, 0)))


def kernel(x_prompt, x_sample, cache_k, cache_v, state_pool, state_conv, page_table, norm1_g, w_in, pool_w, pool_scale, lam_q1, lam_k1, lam_q2, lam_k2, subln_g, w_out, norm2_g, w_up, conv_w, conv_b, w_down, final_g):
    B, S, _ = x_prompt.shape
    DB, T, _ = x_sample.shape
    depth = w_in.shape[0]
    n_pool, page = cache_k.shape[1], cache_k.shape[2]
    past = page_table.shape[1] * page
    assert T == 1 and S % ROW_TILE == 0 and page_table.shape[1] % PAGES_PER_STEP == 0

    cos_p, sin_p = _rope_tables(jnp.arange(S, dtype=jnp.int32))
    cos_s, sin_s = _rope_tables(past + jnp.arange(T, dtype=jnp.int32))
    ck = cache_k.reshape(depth, n_pool, page * N_HEADS, VDIM)
    cv = cache_v.reshape(depth, n_pool, page * N_HEADS, VDIM)
    fg = final_g.reshape(1, D_MODEL)

    xp = x_prompt
    xs = x_sample.reshape(DB, D_MODEL)
    cache_rows = None
    outs = [[] for _ in range(6)]
    for l in range(depth):
        li = _lambda_init(l)
        final = l == depth - 1
        g1 = norm1_g[l].reshape(1, D_MODEL)
        g2 = norm2_g[l].reshape(1, D_MODEL)
        w_in_b = w_in[l].astype(BF16)
        pool_w_b = pool_w[l].astype(BF16)
        ps = pool_scale[l].reshape(1, POOL_W)
        lams = [v[l].reshape(1, HEAD_DIM) for v in (lam_q1, lam_k1, lam_q2, lam_k2)]
        w_out_b = w_out[l].astype(BF16)
        wa = _chunk_cols(w_up[l][:, :D_FF]).astype(BF16)
        wb = _chunk_cols(w_up[l][:, D_FF:]).astype(BF16)
        cw = _chunk_cols(conv_w[l])
        cb = _chunk_cols(conv_b[l].reshape(1, D_FF))
        wd2 = w_down[l].astype(BF16)
        wd = wd2.reshape(N_FF_CHUNKS, FF_CHUNK, D_MODEL)

        q, k, vt, kf, vf, py, tail = _inproj(xp, g1, w_in_b, cos_p, sin_p, pool_w_b, ps,
                                             l, depth, cache_rows)
        cache_rows = (kf, vf)
        ay = _attention(q, k, vt, lams, subln_g[l].reshape(VDIM, 1), li)
        xp, ctail = _ffn(xp, py, ay, w_out_b, g2, wa, wb, cw, cb, wd2, fg, final)
        outs[0].append(tail[:, POOL_CARRY - POOL_HIST:])
        outs[1].append(ctail[:, :, CARRY_ROWS - (CONV_W - 1):, :].transpose(0, 2, 1, 3)
                       .reshape(B, CONV_W - 1, D_FF))

        state_t = state_pool[l].transpose(1, 0, 2)
        qs, ksn, vsn, us, pys = _s_inproj(xs, g1, w_in_b, cos_s, sin_s, state_t, pool_w_b, ps, past)
        ays = _s_attention(page_table, lams, subln_g[l].reshape(1, VDIM),
                           _score_rows(qs), _score_rows(ksn), _score_rows(vsn), ck, cv, l, li)
        prev = _chunk_cols(state_conv[l].reshape(DB * (CONV_W - 1), D_FF)).reshape(
            N_FF_CHUNKS, DB, CONV_W - 1, FF_CHUNK).transpose(0, 2, 1, 3)
        xs, a_s = _s_ffn(xs, pys, ays.reshape(DB, ATTN_W), w_out_b, g2, wa, wb, cw, cb, wd, fg,
                         prev, final)
        a_rows = a_s.transpose(1, 0, 2).reshape(DB, 1, D_FF)
        outs[2].append(ksn.reshape(DB, T, N_HEADS, VDIM))
        outs[3].append(vsn.reshape(DB, T, N_HEADS, VDIM))
        outs[4].append(jnp.concatenate([state_pool[l][:, 1:], us[:, None, :]], axis=1))
        outs[5].append(jnp.concatenate([state_conv[l][:, 1:], a_rows], axis=1))

    kf, vf = cache_rows
    k_new_prompt = kf.reshape(depth, B, S // page, page, N_HEADS, VDIM)
    v_new_prompt = vf.reshape(depth, B, S // page, page, N_HEADS, VDIM)
    y_prompt = xp
    y_sample = xs.reshape(DB, T, D_MODEL)
    stk = [jnp.stack(o) for o in outs]
    return (y_prompt, y_sample, k_new_prompt, v_new_prompt, stk[0], stk[1],
            stk[2], stk[3], stk[4], stk[5])
```

```python
import functools
import math

import jax
import jax.numpy as jnp
from jax import lax
from jax.experimental import pallas as pl
from jax.experimental.pallas import tpu as pltpu

F32 = jnp.float32
BF16 = jnp.bfloat16

D_MODEL = 1024
POOL_W = 512
POOL_WINDOWS = (2, 4, 8, 16)
POOL_GC = 128
POOL_HIST = 15
ATTN_W = 512
HEAD_DIM = 64
VDIM = 128
N_HEADS = 4
IN_W = POOL_W + 3 * ATTN_W
D_FF = 2816
CONV_W = 3
ROPE_THETA = 10000.0
EPS = 1e-6
Q_SCALE = HEAD_DIM ** -0.5 * math.log2(math.e)
NEG_BIG = -1e30
MIN_DENOM = 2.0 ** -100

ROW_TILE = 512
FF_CHUNK = 256
N_FF_CHUNKS = D_FF // FF_CHUNK
CARRY_ROWS = 8
POOL_CARRY = 16
HEADS_PER_STEP = 4
ATTN_Q_TILE = 512
PAGES_PER_STEP = 16
SCORE_ROWS = 16
VT_ROWS = VDIM + 16
VMEM_LIMIT = 56 * 1024 * 1024


def _lambda_init(layer):
    return 0.8 - 0.6 * math.exp(-0.3 * layer)


def _rms(x, g):
    return x * lax.rsqrt(jnp.mean(x * x, axis=-1, keepdims=True) + EPS) * g


def _rope(x, cos, sin_signed):
    lane = lax.broadcasted_iota(jnp.int32, x.shape, 1)
    rot = jnp.where((lane % HEAD_DIM) < HEAD_DIM // 2,
                    pltpu.roll(x, VDIM - HEAD_DIM // 2, 1), pltpu.roll(x, HEAD_DIM // 2, 1))
    return x * cos + rot * sin_signed


def _lam(lq1, lk1, lq2, lk2, lam_init):
    return (jnp.exp(jnp.sum(lq1[...] * lk1[...], axis=-1, keepdims=True))
            - jnp.exp(jnp.sum(lq2[...] * lk2[...], axis=-1, keepdims=True)) + lam_init)


def _inproj_kernel(x_ref, g_ref, w_ref, cos_ref, sin_ref, pw_ref, ps_ref, *rest):
    q_ref, k_ref, vt_ref, kf_ref, vf_ref, py_ref, tail_ref, qn_ref, kmax_ref, ubuf = rest[-10:]
    i = pl.program_id(1)
    tm = x_ref.shape[1]

    @pl.when(i == 0)
    def _():
        ubuf[0:POOL_CARRY, :] = jnp.zeros((POOL_CARRY, POOL_W), F32)

    h = _rms(x_ref[0], g_ref[...]).astype(BF16)
    cos = cos_ref[...]
    sin = sin_ref[...]

    u = jnp.dot(h, w_ref[:, 0:POOL_W], preferred_element_type=F32)
    ubuf[POOL_CARRY:POOL_CARRY + tm, :] = u

    q = jnp.dot(h, w_ref[:, POOL_W:POOL_W + ATTN_W], preferred_element_type=F32)
    k = jnp.dot(h, w_ref[:, POOL_W + ATTN_W:POOL_W + 2 * ATTN_W], preferred_element_type=F32)
    v = jnp.dot(h, w_ref[:, POOL_W + 2 * ATTN_W:IN_W], preferred_element_type=F32)
    for hd in range(N_HEADS):
        sl = slice(hd * VDIM, (hd + 1) * VDIM)
        q_ref[0, :, sl] = (_rope(q[:, sl], cos, sin) * Q_SCALE).astype(BF16)
        kh = _rope(k[:, sl], cos, sin)
        k_ref[0, :, sl] = kh.astype(BF16)
        vh = v[:, sl]
        vt_ref[0, hd, 0, 0:VDIM, :] = vh.T.astype(BF16)
        vt_ref[0, hd, 0, VDIM:VT_ROWS, :] = jnp.ones((VT_ROWS - VDIM, tm), BF16)
        kf_ref[pl.ds(hd, tm, stride=N_HEADS), :] = kh
        vf_ref[pl.ds(hd, tm, stride=N_HEADS), :] = vh

    row = lax.broadcasted_iota(jnp.int32, (ATTN_W, VDIM), 0)
    col = lax.broadcasted_iota(jnp.int32, (ATTN_W, VDIM), 1)
    sel = jnp.where((row // HEAD_DIM) == col, 1.0, 0.0).astype(BF16)

    def chain_norms(ref):
        sq = jnp.square(ref[0].astype(F32)).astype(BF16)
        return jnp.dot(sq, sel, preferred_element_type=F32)

    qn_ref[0] = jnp.sqrt(chain_norms(q_ref).T[0:2 * N_HEADS, :])
    kmx = jnp.max(chain_norms(k_ref), axis=0, keepdims=True)

    @pl.when(i == 0)
    def _():
        kmax_ref[0] = kmx

    @pl.when(i > 0)
    def _():
        kmax_ref[0] = jnp.maximum(kmax_ref[0], kmx)

    pos = i * tm + lax.broadcasted_iota(jnp.int32, (tm, 1), 0)
    for g, w in enumerate(POOL_WINDOWS):
        sl = slice(g * POOL_GC, (g + 1) * POOL_GC)
        ug = u[:, sl]
        acc = ug
        for d in range(1, w):
            acc = acc + ubuf[POOL_CARRY - d:POOL_CARRY - d + tm, sl]
        cnt = jnp.minimum(pos + 1, w).astype(F32)
        dlt = (acc / cnt - ug).astype(BF16)
        y = jnp.dot(dlt, pw_ref[g], preferred_element_type=F32) * ps_ref[:, sl]
        py_ref[0, :, sl] = y.astype(BF16)

    tail = ubuf[tm:tm + POOL_CARRY, :]
    tail_ref[0] = tail
    ubuf[0:POOL_CARRY, :] = tail


def _inproj(x, g1, w_in, cos, sin, pool_w, pool_scale, layer, depth, cache_rows):
    B, S, _ = x.shape
    tm = ROW_TILE
    nb = S // tm
    row = lambda b, i: (b, i, 0)
    const2 = lambda b, i: (0, 0)
    cache_spec = pl.BlockSpec((None, None, N_HEADS * tm, VDIM), lambda b, i: (layer, b, i, 0))
    cache_shape = jax.ShapeDtypeStruct((depth, B, N_HEADS * S, VDIM), F32)
    in_specs = [
        pl.BlockSpec((1, tm, D_MODEL), row),
        pl.BlockSpec((1, D_MODEL), const2),
        pl.BlockSpec((D_MODEL, IN_W), const2),
        pl.BlockSpec((tm, VDIM), lambda b, i: (i, 0)),
        pl.BlockSpec((tm, VDIM), lambda b, i: (i, 0)),
        pl.BlockSpec((len(POOL_WINDOWS), POOL_GC, POOL_GC), lambda b, i: (0, 0, 0)),
        pl.BlockSpec((1, POOL_W), const2),
    ]
    args = [x, g1, w_in, cos, sin, pool_w, pool_scale]
    aliases = {}
    if cache_rows is not None:
        in_specs += [pl.BlockSpec(memory_space=pl.ANY)] * 2
        aliases = {len(args): 3, len(args) + 1: 4}
        args += list(cache_rows)
    return pl.pallas_call(
        _inproj_kernel,
        grid=(B, nb),
        in_specs=in_specs,
        out_specs=[
            pl.BlockSpec((1, tm, ATTN_W), row),
            pl.BlockSpec((1, tm, ATTN_W), row),
            pl.BlockSpec((1, N_HEADS, 1, VT_ROWS, tm), lambda b, i: (b, 0, i, 0, 0)),
            cache_spec,
            cache_spec,
            pl.BlockSpec((1, tm, POOL_W), row),
            pl.BlockSpec((1, POOL_CARRY, POOL_W), lambda b, i: (b, 0, 0)),
            pl.BlockSpec((1, 2 * N_HEADS, tm), lambda b, i: (b, 0, i)),
            pl.BlockSpec((1, 1, VDIM), lambda b, i: (b, 0, 0)),
        ],
        out_shape=[
            jax.ShapeDtypeStruct((B, S, ATTN_W), BF16),
            jax.ShapeDtypeStruct((B, S, ATTN_W), BF16),
            jax.ShapeDtypeStruct((B, N_HEADS, nb, VT_ROWS, tm), BF16),
            cache_shape,
            cache_shape,
            jax.ShapeDtypeStruct((B, S, POOL_W), BF16),
            jax.ShapeDtypeStruct((B, POOL_CARRY, POOL_W), F32),
            jax.ShapeDtypeStruct((B, 2 * N_HEADS, S), F32),
            jax.ShapeDtypeStruct((B, 1, VDIM), F32),
        ],
        input_output_aliases=aliases,
        scratch_shapes=[pltpu.VMEM((POOL_CARRY + tm, POOL_W), F32)],
        compiler_params=pltpu.CompilerParams(
            dimension_semantics=("arbitrary", "arbitrary"), vmem_limit_bytes=VMEM_LIMIT),
        name="prompt_inproj",
    )(*args)


def _attn_kernel(lq1, lk1, lq2, lk2, g_ref, q_ref, k_ref, vt_ref, qn_ref, kmax_ref, o_ref,
                 m_ref, acc_ref, *, lam_init):
    hp = pl.program_id(1)
    qi = pl.program_id(2)
    tq = q_ref.shape[1]
    tk = vt_ref.shape[4]
    n_chain = 2 * HEADS_PER_STEP
    qz = []
    for hl in range(HEADS_PER_STEP):
        q = q_ref[0, :, hl * VDIM:(hl + 1) * VDIM].astype(F32)
        lane = lax.broadcasted_iota(jnp.int32, q.shape, 1)
        qz.append(jnp.where(lane < HEAD_DIM, q, 0.0).astype(BF16))
        qz.append(jnp.where(lane >= HEAD_DIM, q, 0.0).astype(BF16))

    klane = lax.broadcasted_iota(jnp.int32, (1, VDIM), 1)
    bound = []
    for ch in range(n_chain):
        km2 = jnp.max(jnp.where(klane == n_chain * hp + ch, kmax_ref[0], 0.0),
                      axis=1, keepdims=True)
        bound.append(qn_ref[0, 0, ch:ch + 1, :] * jnp.sqrt(km2))

    def scores(j, ch, masked):
        ks = pl.multiple_of(j * tk, tk)
        kb = k_ref[0, pl.ds(ks, tk), (ch // 2) * VDIM:(ch // 2 + 1) * VDIM]
        sc = lax.dot_general(kb, qz[ch], (((1,), (1,)), ((), ())),
                             preferred_element_type=F32)
        if masked:
            key = j * tk + lax.broadcasted_iota(jnp.int32, sc.shape, 0)
            qry = qi * tq + lax.broadcasted_iota(jnp.int32, sc.shape, 1)
            sc = jnp.where(key <= qry, sc, NEG_BIG)
        return sc

    def bounded_block(j, masked):
        s = {0: scores(j, 0, masked)}
        for ch in range(n_chain):
            if ch + 1 < n_chain:
                s[ch + 1] = scores(j, ch + 1, masked)
            p = jnp.exp2(s.pop(ch) - bound[ch]).astype(BF16)
            acc_ref[ch] += jnp.dot(vt_ref[0, ch // 2, j], p, preferred_element_type=F32)

    def running_max_block(j, masked):
        s = [scores(j, ch, masked) for ch in range(n_chain)]
        p = []
        alpha = []
        for ch in range(n_chain):
            m_old = m_ref[ch]
            m_new = jnp.maximum(m_old, jnp.max(s[ch], axis=0, keepdims=True))
            alpha.append(jnp.exp2(m_old - m_new))
            p.append(jnp.exp2((s[ch] - m_new).astype(BF16)))
            m_ref[ch] = m_new
        for ch in range(n_chain):
            pv = jnp.dot(vt_ref[0, ch // 2, j], p[ch], preferred_element_type=F32)
            acc_ref[ch] = acc_ref[ch] * alpha[ch] + pv

    def all_blocks(block):
        acc_ref[...] = jnp.zeros(acc_ref.shape, F32)
        n_full = (qi * tq) // tk

        def body(j, carry):
            block(j, False)
            return carry

        lax.fori_loop(0, n_full, body, 0)
        block(n_full, True)

    all_blocks(bounded_block)
    denom_min = jnp.min(jnp.concatenate(
        [acc_ref[ch, VDIM:VDIM + 1, :] for ch in range(n_chain)], axis=0))

    @pl.when(jnp.logical_not(denom_min >= MIN_DENOM))
    def _():
        m_ref[...] = jnp.full(m_ref.shape, NEG_BIG, F32)
        all_blocks(running_max_block)

    lam = _lam(lq1, lk1, lq2, lk2, lam_init)
    for hl in range(HEADS_PER_STEP):
        a1 = acc_ref[2 * hl]
        a2 = acc_ref[2 * hl + 1]
        o = (a1[0:VDIM] / a1[VDIM:VDIM + 1] - lam * (a2[0:VDIM] / a2[VDIM:VDIM + 1]))
        y = o * lax.rsqrt(jnp.mean(o * o, axis=0, keepdims=True) + EPS) * g_ref[...]
        o_ref[0, :, hl * VDIM:(hl + 1) * VDIM] = (y * (1.0 - lam_init)).T.astype(BF16)


def _attention(q, k, vt, qn, kmax, lams, subln_col, lam_init):
    B, S, _ = q.shape
    tq = ATTN_Q_TILE
    tk = vt.shape[4]
    nkb = vt.shape[2]
    hs = HEADS_PER_STEP
    vec = pl.BlockSpec((1, HEAD_DIM), lambda b, h, i: (0, 0))
    return pl.pallas_call(
        functools.partial(_attn_kernel, lam_init=lam_init),
        grid=(B, N_HEADS // hs, S // tq),
        in_specs=[
            vec, vec, vec, vec,
            pl.BlockSpec((VDIM, 1), lambda b, h, i: (0, 0)),
            pl.BlockSpec((1, tq, hs * VDIM), lambda b, h, i: (b, i, h)),
            pl.BlockSpec((1, S, hs * VDIM), lambda b, h, i: (b, 0, h)),
            pl.BlockSpec((1, hs, nkb, VT_ROWS, tk), lambda b, h, i: (b, h, 0, 0, 0)),
            pl.BlockSpec((1, 1, 2 * hs, tq), lambda b, h, i: (b, h, 0, i)),
            pl.BlockSpec((1, 1, VDIM), lambda b, h, i: (b, 0, 0)),
        ],
        out_specs=pl.BlockSpec((1, tq, hs * VDIM), lambda b, h, i: (b, i, h)),
        out_shape=jax.ShapeDtypeStruct((B, S, ATTN_W), BF16),
        scratch_shapes=[
            pltpu.VMEM((2 * hs, 1, tq), F32),
            pltpu.VMEM((2 * hs, VT_ROWS, tq), F32),
        ],
        compiler_params=pltpu.CompilerParams(
            dimension_semantics=("arbitrary", "arbitrary", "arbitrary"),
            vmem_limit_bytes=VMEM_LIMIT),
        name="prompt_attention",
    )(*lams, subln_col, q, k, vt, qn.reshape(B, N_HEADS // hs, 2 * hs, S), kmax)


def _ffn_gate(j, a, bg, a2, a1, cw_ref, cb_ref):
    cw = cw_ref[j]
    c = cw[0:1] * a2 + cw[1:2] * a1 + cw[2:3] * a + cb_ref[j]
    return ((c * jax.nn.sigmoid(c)) * bg).astype(BF16)


def _ffn_kernel(x_ref, py_ref, ay_ref, wo_ref, g2_ref, wa_ref, wb_ref, cw_ref, cb_ref, wd_ref,
                fg_ref, out_ref, ctail_ref, carry, abuf, bbuf, h2_ref, acc_ref, gbuf, *, final):
    i = pl.program_id(1)
    tm = x_ref.shape[1]

    @pl.when(i == 0)
    def _():
        carry[...] = jnp.zeros(carry.shape, F32)

    x1 = (x_ref[0]
          + jnp.dot(py_ref[0], wo_ref[0:POOL_W, :], preferred_element_type=F32)
          + jnp.dot(ay_ref[0], wo_ref[POOL_W:POOL_W + ATTN_W, :], preferred_element_type=F32))
    h2_ref[...] = _rms(x1, g2_ref[...]).astype(BF16)

    def up(j):
        h2 = h2_ref[...]
        abuf[j % 2, CARRY_ROWS:CARRY_ROWS + tm, :] = jnp.dot(
            h2, wa_ref[j], preferred_element_type=F32)
        bbuf[j % 2] = jnp.dot(h2, wb_ref[j], preferred_element_type=F32)

    acc_ref[...] = x1
    up(0)
    for j in range(N_FF_CHUNKS):
        if j + 1 < N_FF_CHUNKS:
            up(j + 1)
        ab = abuf.at[j % 2]
        ab[0:CARRY_ROWS, :] = carry[j]
        carry[j] = ab[tm:tm + CARRY_ROWS, :]
        gated = _ffn_gate(j, ab[CARRY_ROWS:CARRY_ROWS + tm, :], bbuf[j % 2],
                          ab[CARRY_ROWS - 2:CARRY_ROWS - 2 + tm, :],
                          ab[CARRY_ROWS - 1:CARRY_ROWS - 1 + tm, :], cw_ref, cb_ref)
        gbuf[:, j * FF_CHUNK:(j + 1) * FF_CHUNK] = gated

    acc = acc_ref[...] + jnp.dot(gbuf[...], wd_ref[...], preferred_element_type=F32)
    if final:
        acc = _rms(acc, fg_ref[...])
    out_ref[0] = acc
    ctail_ref[0] = carry[...]


def _ffn(x, py, ay, w_out, g2, wa, wb, cw, cb, wd, fg, final):
    B, S, _ = x.shape
    tm = ROW_TILE
    nb = S // tm
    row = lambda b, i: (b, i, 0)
    const2 = lambda b, i: (0, 0)
    const3 = lambda b, i: (0, 0, 0)
    resident = dict(pipeline_mode=pl.Buffered(1))
    return pl.pallas_call(
        functools.partial(_ffn_kernel, final=final),
        grid=(B, nb),
        in_specs=[
            pl.BlockSpec((1, tm, D_MODEL), row),
            pl.BlockSpec((1, tm, POOL_W), row),
            pl.BlockSpec((1, tm, ATTN_W), row),
            pl.BlockSpec((D_MODEL, D_MODEL), const2, **resident),
            pl.BlockSpec((1, D_MODEL), const2),
            pl.BlockSpec((N_FF_CHUNKS, D_MODEL, FF_CHUNK), const3, **resident),
            pl.BlockSpec((N_FF_CHUNKS, D_MODEL, FF_CHUNK), const3, **resident),
            pl.BlockSpec((N_FF_CHUNKS, CONV_W, FF_CHUNK), const3),
            pl.BlockSpec((N_FF_CHUNKS, 1, FF_CHUNK), const3),
            pl.BlockSpec((D_FF, D_MODEL), const2, **resident),
            pl.BlockSpec((1, D_MODEL), const2),
        ],
        out_specs=[
            pl.BlockSpec((1, tm, D_MODEL), row),
            pl.BlockSpec((1, N_FF_CHUNKS, CARRY_ROWS, FF_CHUNK), lambda b, i: (b, 0, 0, 0)),
        ],
        out_shape=[
            jax.ShapeDtypeStruct((B, S, D_MODEL), F32),
            jax.ShapeDtypeStruct((B, N_FF_CHUNKS, CARRY_ROWS, FF_CHUNK), F32),
        ],
        scratch_shapes=[
            pltpu.VMEM((N_FF_CHUNKS, CARRY_ROWS, FF_CHUNK), F32),
            pltpu.VMEM((2, CARRY_ROWS + tm, FF_CHUNK), F32),
            pltpu.VMEM((2, tm, FF_CHUNK), F32),
            pltpu.VMEM((tm, D_MODEL), BF16),
            pltpu.VMEM((tm, D_MODEL), F32),
            pltpu.VMEM((tm, D_FF), BF16),
        ],
        compiler_params=pltpu.CompilerParams(
            dimension_semantics=("arbitrary", "arbitrary"), vmem_limit_bytes=VMEM_LIMIT),
        name="prompt_ffn",
    )(x, py, ay, w_out, g2, wa, wb, cw, cb, wd, fg)


def _s_inproj_kernel(x_ref, g_ref, w_ref, cos_ref, sin_ref, st_ref, pw_ref, ps_ref,
                     q_ref, k_ref, v_ref, u_ref, py_ref, *, pos):
    h = _rms(x_ref[...], g_ref[...]).astype(BF16)
    proj = jnp.dot(h, w_ref[...], preferred_element_type=F32)
    u = proj[:, 0:POOL_W]
    u_ref[...] = u
    cos = cos_ref[...]
    sin = sin_ref[...]
    for hd in range(N_HEADS):
        sl = slice(hd * VDIM, (hd + 1) * VDIM)
        q_ref[:, sl] = _rope(proj[:, POOL_W + hd * VDIM:POOL_W + (hd + 1) * VDIM], cos, sin)
        k_ref[:, sl] = _rope(
            proj[:, POOL_W + ATTN_W + hd * VDIM:POOL_W + ATTN_W + (hd + 1) * VDIM], cos, sin)
    v_ref[...] = proj[:, POOL_W + 2 * ATTN_W:IN_W]
    for g, w in enumerate(POOL_WINDOWS):
        sl = slice(g * POOL_GC, (g + 1) * POOL_GC)
        ug = u[:, sl]
        acc = ug
        for d in range(1, w):
            acc = acc + st_ref[POOL_HIST - d, :, sl]
        cnt = float(min(pos + 1, w))
        dlt = (acc / cnt - ug).astype(BF16)
        y = jnp.dot(dlt, pw_ref[g], preferred_element_type=F32) * ps_ref[:, sl]
        py_ref[:, sl] = y.astype(BF16)


def _s_inproj(x, g1, w_in, cos, sin, state_t, pool_w, pool_scale, pos):
    db = x.shape[0]
    return pl.pallas_call(
        functools.partial(_s_inproj_kernel, pos=pos),
        out_shape=[
            jax.ShapeDtypeStruct((db, ATTN_W), F32),
            jax.ShapeDtypeStruct((db, ATTN_W), F32),
            jax.ShapeDtypeStruct((db, ATTN_W), F32),
            jax.ShapeDtypeStruct((db, POOL_W), F32),
            jax.ShapeDtypeStruct((db, POOL_W), BF16),
        ],
        compiler_params=pltpu.CompilerParams(vmem_limit_bytes=VMEM_LIMIT),
        name="sample_inproj",
    )(x, g1, w_in, cos, sin, state_t, pool_w, pool_scale)


def _s_attn_kernel(pt_ref, lq1, lk1, lq2, lk2, g_ref, q_ref, kn_ref, vn_ref, *rest, lam_init):
    np_ = PAGES_PER_STEP
    k_refs = rest[:np_]
    v_refs = rest[np_:2 * np_]
    o_ref, m_ref, l_ref, acc_ref = rest[2 * np_:]
    j = pl.program_id(1)
    rows = SCORE_ROWS
    prow = k_refs[0].shape[0]

    @pl.when(j == 0)
    def _():
        m_ref[...] = jnp.full(m_ref.shape, NEG_BIG, F32)
        l_ref[...] = jnp.zeros(l_ref.shape, F32)
        acc_ref[...] = jnp.zeros(acc_ref.shape, F32)

    r = lax.broadcasted_iota(jnp.int32, (rows, VDIM), 0)
    lane = lax.broadcasted_iota(jnp.int32, (rows, VDIM), 1)
    qm = jnp.where((lane // HEAD_DIM) == (r % 2), q_ref[0] * Q_SCALE, 0.0).astype(BF16)

    rr = lax.broadcasted_iota(jnp.int32, (rows, prow), 0)
    col = lax.broadcasted_iota(jnp.int32, (rows, prow), 1)
    own = (col % N_HEADS) == (rr // 2)
    s = [jnp.where(own, lax.dot_general(qm, kr[...].astype(BF16), (((1,), (1,)), ((), ())),
                                        preferred_element_type=F32), NEG_BIG)
         for kr in k_refs]
    m_blk = s[0]
    for sp in s[1:]:
        m_blk = jnp.maximum(m_blk, sp)
    m_old = m_ref[...]
    m_new = jnp.maximum(m_old, jnp.max(m_blk, axis=1, keepdims=True))
    alpha = jnp.exp2(m_old - m_new)
    p = [jnp.exp2(sp - m_new).astype(BF16) for sp in s]
    l_blk = p[0].astype(F32)
    for pp in p[1:]:
        l_blk = l_blk + pp.astype(F32)
    l_ref[...] = alpha * l_ref[...] + jnp.sum(l_blk, axis=1, keepdims=True)
    pv = jnp.zeros((rows, VDIM), F32)
    for pp, vr in zip(p, v_refs):
        pv = pv + jnp.dot(pp, vr[...].astype(BF16), preferred_element_type=F32)
    acc_ref[...] = alpha * acc_ref[...] + pv
    m_ref[...] = m_new

    @pl.when(j == pl.num_programs(1) - 1)
    def _():
        kn = kn_ref[0].astype(BF16).astype(F32)
        vn = vn_ref[0].astype(BF16).astype(F32)
        s_new = jnp.sum(qm.astype(F32) * kn, axis=1, keepdims=True)
        m_o = m_ref[...]
        m_n = jnp.maximum(m_o, s_new)
        a_n = jnp.exp2(m_o - m_n)
        p_n = jnp.exp2(s_new - m_n)
        l_n = a_n * l_ref[...] + p_n
        on = (a_n * acc_ref[...] + p_n.astype(BF16).astype(F32) * vn) / l_n
        lam = _lam(lq1, lk1, lq2, lk2, lam_init)
        o = jnp.concatenate(
            [on[2 * hd:2 * hd + 1, :] - lam * on[2 * hd + 1:2 * hd + 2, :]
             for hd in range(N_HEADS)], axis=0)
        y = o * lax.rsqrt(jnp.mean(o * o, axis=-1, keepdims=True) + EPS) * g_ref[...]
        o_ref[0] = y * (1.0 - lam_init)


def _s_attention(page_table, lams, subln_row, q16, kn16, vn16, cache_k, cache_v, layer, lam_init):
    db, n_pages = page_table.shape
    np_ = PAGES_PER_STEP
    prow = cache_k.shape[2]
    rows = SCORE_ROWS
    vec = pl.BlockSpec((1, HEAD_DIM), lambda b, j, pt: (0, 0))
    tok = pl.BlockSpec((1, rows, VDIM), lambda b, j, pt: (b, 0, 0))

    def page_spec(idx):
        return pl.BlockSpec((None, None, prow, VDIM),
                            lambda b, j, pt: (layer, pt[b, j * np_ + idx], 0, 0))

    grid_spec = pltpu.PrefetchScalarGridSpec(
        num_scalar_prefetch=1,
        grid=(db, n_pages // np_),
        in_specs=[vec, vec, vec, vec,
                  pl.BlockSpec((1, VDIM), lambda b, j, pt: (0, 0)),
                  tok, tok, tok]
                 + [page_spec(idx) for idx in range(np_)]
                 + [page_spec(idx) for idx in range(np_)],
        out_specs=pl.BlockSpec((1, N_HEADS, VDIM), lambda b, j, pt: (b, 0, 0)),
        scratch_shapes=[
            pltpu.VMEM((rows, 1), F32),
            pltpu.VMEM((rows, 1), F32),
            pltpu.VMEM((rows, VDIM), F32),
        ],
    )
    return pl.pallas_call(
        functools.partial(_s_attn_kernel, lam_init=lam_init),
        grid_spec=grid_spec,
        out_shape=jax.ShapeDtypeStruct((db, N_HEADS, VDIM), F32),
        compiler_params=pltpu.CompilerParams(
            dimension_semantics=("arbitrary", "arbitrary"), vmem_limit_bytes=VMEM_LIMIT),
        name="sample_attention",
    )(page_table, *lams, subln_row, q16, kn16, vn16, *([cache_k] * np_), *([cache_v] * np_))


def _s_ffn_kernel(x_ref, py_ref, ay_ref, wo_ref, g2_ref, wa_ref, wb_ref, cw_ref, cb_ref, wd_ref,
                  fg_ref, prev_ref, out_ref, a_ref, h2_ref, acc_ref, *, final):
    x1 = (x_ref[...]
          + jnp.dot(py_ref[...], wo_ref[0:POOL_W, :], preferred_element_type=F32)
          + jnp.dot(ay_ref[...].astype(BF16), wo_ref[POOL_W:POOL_W + ATTN_W, :],
                    preferred_element_type=F32))
    h2_ref[...] = _rms(x1, g2_ref[...]).astype(BF16)
    acc_ref[...] = x1

    def chunk(j, c):
        h2 = h2_ref[...]
        a = jnp.dot(h2, wa_ref[j], preferred_element_type=F32)
        bg = jnp.dot(h2, wb_ref[j], preferred_element_type=F32)
        a_ref[j] = a
        gated = _ffn_gate(j, a, bg, prev_ref[j, 0], prev_ref[j, 1], cw_ref, cb_ref)
        acc_ref[...] += jnp.dot(gated, wd_ref[j], preferred_element_type=F32)
        return c

    lax.fori_loop(0, N_FF_CHUNKS, chunk, 0)

    out = acc_ref[...]
    if final:
        out = _rms(out, fg_ref[...])
    out_ref[...] = out


def _s_ffn(x, py, ay, w_out, g2, wa, wb, cw, cb, wd, fg, prev, final):
    db = x.shape[0]
    return pl.pallas_call(
        functools.partial(_s_ffn_kernel, final=final),
        out_shape=[
            jax.ShapeDtypeStruct((db, D_MODEL), F32),
            jax.ShapeDtypeStruct((N_FF_CHUNKS, db, FF_CHUNK), F32),
        ],
        scratch_shapes=[
            pltpu.VMEM((db, D_MODEL), BF16),
            pltpu.VMEM((db, D_MODEL), F32),
        ],
        compiler_params=pltpu.CompilerParams(vmem_limit_bytes=VMEM_LIMIT),
        name="sample_ffn",
    )(x, py, ay, w_out, g2, wa, wb, cw, cb, wd, fg, prev)


def _rope_tables(pos):
    half = HEAD_DIM // 2
    inv = 1.0 / (ROPE_THETA ** (jnp.arange(half, dtype=F32) * (2.0 / HEAD_DIM)))
    ang = pos.astype(F32)[:, None] * inv[None, :]
    cos = jnp.cos(ang)
    sin = jnp.sin(ang)
    return (jnp.concatenate([cos, cos, cos, cos], axis=1),
            jnp.concatenate([-sin, sin, -sin, sin], axis=1))


def _chunk_cols(w):
    return w.reshape(w.shape[0], N_FF_CHUNKS, FF_CHUNK).transpose(1, 0, 2)


def _score_rows(x):
    db = x.shape[0]
    x = jnp.repeat(x.reshape(db, N_HEADS, VDIM), 2, axis=1)
    return jnp.pad(x, ((0, 0), (0, SCORE_ROWS - 2 * N_HEADS), (0, 0)))


def kernel(x_prompt, x_sample, cache_k, cache_v, state_pool, state_conv, page_table, norm1_g, w_in, pool_w, pool_scale, lam_q1, lam_k1, lam_q2, lam_k2, subln_g, w_out, norm2_g, w_up, conv_w, conv_b, w_down, final_g):
    B, S, _ = x_prompt.shape
    DB, T, _ = x_sample.shape
    depth = w_in.shape[0]
    n_pool, page = cache_k.shape[1], cache_k.shape[2]
    past = page_table.shape[1] * page
    assert T == 1 and S % ROW_TILE == 0 and page_table.shape[1] % PAGES_PER_STEP == 0

    cos_p, sin_p = _rope_tables(jnp.arange(S, dtype=jnp.int32))
    cos_s, sin_s = _rope_tables(past + jnp.arange(T, dtype=jnp.int32))
    ck = cache_k.reshape(depth, n_pool, page * N_HEADS, VDIM)
    cv = cache_v.reshape(depth, n_pool, page * N_HEADS, VDIM)
    fg = final_g.reshape(1, D_MODEL)

    xp = x_prompt
    xs = x_sample.reshape(DB, D_MODEL)
    cache_rows = None
    outs = [[] for _ in range(6)]
    for l in range(depth):
        li = _lambda_init(l)
        final = l == depth - 1
        g1 = norm1_g[l].reshape(1, D_MODEL)
        g2 = norm2_g[l].reshape(1, D_MODEL)
        w_in_b = w_in[l].astype(BF16)
        pool_w_b = pool_w[l].astype(BF16)
        ps = pool_scale[l].reshape(1, POOL_W)
        lams = [v[l].reshape(1, HEAD_DIM) for v in (lam_q1, lam_k1, lam_q2, lam_k2)]
        w_out_b = w_out[l].astype(BF16)
        wa = _chunk_cols(w_up[l][:, :D_FF]).astype(BF16)
        wb = _chunk_cols(w_up[l][:, D_FF:]).astype(BF16)
        cw = _chunk_cols(conv_w[l])
        cb = _chunk_cols(conv_b[l].reshape(1, D_FF))
        wd2 = w_down[l].astype(BF16)
        wd = wd2.reshape(N_FF_CHUNKS, FF_CHUNK, D_MODEL)

        q, k, vt, kf, vf, py, tail, qn, kmax = _inproj(xp, g1, w_in_b, cos_p, sin_p, pool_w_b,
                                                       ps, l, depth, cache_rows)
        cache_rows = (kf, vf)
        ay = _attention(q, k, vt, qn, kmax, lams, subln_g[l].reshape(VDIM, 1), li)
        xp, ctail = _ffn(xp, py, ay, w_out_b, g2, wa, wb, cw, cb, wd2, fg, final)
        outs[0].append(tail[:, POOL_CARRY - POOL_HIST:])
        outs[1].append(ctail[:, :, CARRY_ROWS - (CONV_W - 1):, :].transpose(0, 2, 1, 3)
                       .reshape(B, CONV_W - 1, D_FF))

        state_t = state_pool[l].transpose(1, 0, 2)
        qs, ksn, vsn, us, pys = _s_inproj(xs, g1, w_in_b, cos_s, sin_s, state_t, pool_w_b, ps, past)
        ays = _s_attention(page_table, lams, subln_g[l].reshape(1, VDIM),
                           _score_rows(qs), _score_rows(ksn), _score_rows(vsn), ck, cv, l, li)
        prev = _chunk_cols(state_conv[l].reshape(DB * (CONV_W - 1), D_FF)).reshape(
            N_FF_CHUNKS, DB, CONV_W - 1, FF_CHUNK).transpose(0, 2, 1, 3)
        xs, a_s = _s_ffn(xs, pys, ays.reshape(DB, ATTN_W), w_out_b, g2, wa, wb, cw, cb, wd, fg,
                         prev, final)
        a_rows = a_s.transpose(1, 0, 2).reshape(DB, 1, D_FF)
        outs[2].append(ksn.reshape(DB, T, N_HEADS, VDIM))
        outs[3].append(vsn.reshape(DB, T, N_HEADS, VDIM))
        outs[4].append(jnp.concatenate([state_pool[l][:, 1:], us[:, None, :]], axis=1))
        outs[5].append(jnp.concatenate([state_conv[l][:, 1:], a_rows], axis=1))

    kf, vf = cache_rows
    k_new_prompt = kf.reshape(depth, B, S // page, page, N_HEADS, VDIM)
    v_new_prompt = vf.reshape(depth, B, S // page, page, N_HEADS, VDIM)
    y_prompt = xp
    y_sample = xs.reshape(DB, T, D_MODEL)
    stk = [jnp.stack(o) for o in outs]
    return (y_prompt, y_sample, k_new_prompt, v_new_prompt, stk[0], stk[1],
            stk[2], stk[3], stk[4], stk[5])
```

```python
import functools
import math

import jax
import jax.numpy as jnp
from jax import lax
from jax.experimental import pallas as pl
from jax.experimental.pallas import tpu as pltpu

F32 = jnp.float32
BF16 = jnp.bfloat16

D_MODEL = 1024
POOL_W = 512
POOL_WINDOWS = (2, 4, 8, 16)
POOL_GC = 128
POOL_HIST = 15
ATTN_W = 512
HEAD_DIM = 64
VDIM = 128
N_HEADS = 4
IN_W = POOL_W + 3 * ATTN_W
D_FF = 2816
CONV_W = 3
ROPE_THETA = 10000.0
EPS = 1e-6
Q_SCALE = HEAD_DIM ** -0.5 * math.log2(math.e)
NEG_BIG = -1e30
MIN_DENOM = 2.0 ** -100

ROW_TILE = 512
FF_CHUNK = 256
N_FF_CHUNKS = D_FF // FF_CHUNK
CARRY_ROWS = 8
POOL_CARRY = 16
HEADS_PER_STEP = 4
ATTN_Q_TILE = 512
PAGES_PER_STEP = 32
SCORE_ROWS = 16
VT_ROWS = VDIM + 16
VMEM_LIMIT = 56 * 1024 * 1024


def _lambda_init(layer):
    return 0.8 - 0.6 * math.exp(-0.3 * layer)


def _rms(x, g):
    return x * lax.rsqrt(jnp.mean(x * x, axis=-1, keepdims=True) + EPS) * g


def _rope(x, cos, sin_signed):
    lane = lax.broadcasted_iota(jnp.int32, x.shape, 1)
    rot = jnp.where((lane % HEAD_DIM) < HEAD_DIM // 2,
                    pltpu.roll(x, VDIM - HEAD_DIM // 2, 1), pltpu.roll(x, HEAD_DIM // 2, 1))
    return x * cos + rot * sin_signed


def _lam(lq1, lk1, lq2, lk2, lam_init):
    return (jnp.exp(jnp.sum(lq1[...] * lk1[...], axis=-1, keepdims=True))
            - jnp.exp(jnp.sum(lq2[...] * lk2[...], axis=-1, keepdims=True)) + lam_init)


def _inproj_kernel(x_ref, g_ref, w_ref, cos_ref, sin_ref, pw_ref, ps_ref, *rest):
    q_ref, k_ref, vt_ref, kf_ref, vf_ref, py_ref, tail_ref, qn_ref, kmax_ref, ubuf = rest[-10:]
    i = pl.program_id(1)
    tm = x_ref.shape[1]

    @pl.when(i == 0)
    def _():
        ubuf[0:POOL_CARRY, :] = jnp.zeros((POOL_CARRY, POOL_W), F32)

    h = _rms(x_ref[0], g_ref[...]).astype(BF16)
    cos = cos_ref[...]
    sin = sin_ref[...]

    u = jnp.dot(h, w_ref[:, 0:POOL_W], preferred_element_type=F32)
    ubuf[POOL_CARRY:POOL_CARRY + tm, :] = u

    q = jnp.dot(h, w_ref[:, POOL_W:POOL_W + ATTN_W], preferred_element_type=F32)
    k = jnp.dot(h, w_ref[:, POOL_W + ATTN_W:POOL_W + 2 * ATTN_W], preferred_element_type=F32)
    v = jnp.dot(h, w_ref[:, POOL_W + 2 * ATTN_W:IN_W], preferred_element_type=F32)
    for hd in range(N_HEADS):
        sl = slice(hd * VDIM, (hd + 1) * VDIM)
        q_ref[0, :, sl] = (_rope(q[:, sl], cos, sin) * Q_SCALE).astype(BF16)
        kh = _rope(k[:, sl], cos, sin)
        k_ref[0, :, sl] = kh.astype(BF16)
        vh = v[:, sl]
        vt_ref[0, hd, 0, 0:VDIM, :] = vh.T.astype(BF16)
        vt_ref[0, hd, 0, VDIM:VT_ROWS, :] = jnp.ones((VT_ROWS - VDIM, tm), BF16)
        kf_ref[pl.ds(hd, tm, stride=N_HEADS), :] = kh
        vf_ref[pl.ds(hd, tm, stride=N_HEADS), :] = vh

    row = lax.broadcasted_iota(jnp.int32, (ATTN_W, VDIM), 0)
    col = lax.broadcasted_iota(jnp.int32, (ATTN_W, VDIM), 1)
    sel = jnp.where((row // HEAD_DIM) == col, 1.0, 0.0).astype(BF16)

    def chain_norms(ref):
        sq = jnp.square(ref[0].astype(F32)).astype(BF16)
        return jnp.dot(sq, sel, preferred_element_type=F32)

    qn_ref[0] = jnp.sqrt(chain_norms(q_ref).T[0:2 * N_HEADS, :])
    kmx = jnp.max(chain_norms(k_ref), axis=0, keepdims=True)

    @pl.when(i == 0)
    def _():
        kmax_ref[0] = kmx

    @pl.when(i > 0)
    def _():
        kmax_ref[0] = jnp.maximum(kmax_ref[0], kmx)

    pos = i * tm + lax.broadcasted_iota(jnp.int32, (tm, 1), 0)
    for g, w in enumerate(POOL_WINDOWS):
        sl = slice(g * POOL_GC, (g + 1) * POOL_GC)
        ug = u[:, sl]
        acc = ug
        for d in range(1, w):
            acc = acc + ubuf[POOL_CARRY - d:POOL_CARRY - d + tm, sl]
        cnt = jnp.minimum(pos + 1, w).astype(F32)
        dlt = (acc / cnt - ug).astype(BF16)
        y = jnp.dot(dlt, pw_ref[g], preferred_element_type=F32) * ps_ref[:, sl]
        py_ref[0, :, sl] = y.astype(BF16)

    tail = ubuf[tm:tm + POOL_CARRY, :]
    tail_ref[0] = tail
    ubuf[0:POOL_CARRY, :] = tail


def _inproj(x, g1, w_in, cos, sin, pool_w, pool_scale, layer, depth, cache_rows):
    B, S, _ = x.shape
    tm = ROW_TILE
    nb = S // tm
    row = lambda b, i: (b, i, 0)
    const2 = lambda b, i: (0, 0)
    cache_spec = pl.BlockSpec((None, None, N_HEADS * tm, VDIM), lambda b, i: (layer, b, i, 0))
    cache_shape = jax.ShapeDtypeStruct((depth, B, N_HEADS * S, VDIM), F32)
    in_specs = [
        pl.BlockSpec((1, tm, D_MODEL), row),
        pl.BlockSpec((1, D_MODEL), const2),
        pl.BlockSpec((D_MODEL, IN_W), const2),
        pl.BlockSpec((tm, VDIM), lambda b, i: (i, 0)),
        pl.BlockSpec((tm, VDIM), lambda b, i: (i, 0)),
        pl.BlockSpec((len(POOL_WINDOWS), POOL_GC, POOL_GC), lambda b, i: (0, 0, 0)),
        pl.BlockSpec((1, POOL_W), const2),
    ]
    args = [x, g1, w_in, cos, sin, pool_w, pool_scale]
    aliases = {}
    if cache_rows is not None:
        in_specs += [pl.BlockSpec(memory_space=pl.ANY)] * 2
        aliases = {len(args): 3, len(args) + 1: 4}
        args += list(cache_rows)
    return pl.pallas_call(
        _inproj_kernel,
        grid=(B, nb),
        in_specs=in_specs,
        out_specs=[
            pl.BlockSpec((1, tm, ATTN_W), row),
            pl.BlockSpec((1, tm, ATTN_W), row),
            pl.BlockSpec((1, N_HEADS, 1, VT_ROWS, tm), lambda b, i: (b, 0, i, 0, 0)),
            cache_spec,
            cache_spec,
            pl.BlockSpec((1, tm, POOL_W), row),
            pl.BlockSpec((1, POOL_CARRY, POOL_W), lambda b, i: (b, 0, 0)),
            pl.BlockSpec((1, 2 * N_HEADS, tm), lambda b, i: (b, 0, i)),
            pl.BlockSpec((1, 1, VDIM), lambda b, i: (b, 0, 0)),
        ],
        out_shape=[
            jax.ShapeDtypeStruct((B, S, ATTN_W), BF16),
            jax.ShapeDtypeStruct((B, S, ATTN_W), BF16),
            jax.ShapeDtypeStruct((B, N_HEADS, nb, VT_ROWS, tm), BF16),
            cache_shape,
            cache_shape,
            jax.ShapeDtypeStruct((B, S, POOL_W), BF16),
            jax.ShapeDtypeStruct((B, POOL_CARRY, POOL_W), F32),
            jax.ShapeDtypeStruct((B, 2 * N_HEADS, S), F32),
            jax.ShapeDtypeStruct((B, 1, VDIM), F32),
        ],
        input_output_aliases=aliases,
        scratch_shapes=[pltpu.VMEM((POOL_CARRY + tm, POOL_W), F32)],
        compiler_params=pltpu.CompilerParams(
            dimension_semantics=("arbitrary", "arbitrary"), vmem_limit_bytes=VMEM_LIMIT),
        name="prompt_inproj",
    )(*args)


def _attn_kernel(lq1, lk1, lq2, lk2, g_ref, q_ref, k_ref, vt_ref, qn_ref, kmax_ref, o_ref,
                 m_ref, acc_ref, *, lam_init):
    hp = pl.program_id(1)
    qi = pl.program_id(2)
    tq = q_ref.shape[1]
    tk = vt_ref.shape[4]
    n_chain = 2 * HEADS_PER_STEP
    qz = []
    for hl in range(HEADS_PER_STEP):
        q = q_ref[0, :, hl * VDIM:(hl + 1) * VDIM].astype(F32)
        lane = lax.broadcasted_iota(jnp.int32, q.shape, 1)
        qz.append(jnp.where(lane < HEAD_DIM, q, 0.0).astype(BF16))
        qz.append(jnp.where(lane >= HEAD_DIM, q, 0.0).astype(BF16))

    klane = lax.broadcasted_iota(jnp.int32, (1, VDIM), 1)
    bound = []
    for ch in range(n_chain):
        km2 = jnp.max(jnp.where(klane == n_chain * hp + ch, kmax_ref[0], 0.0),
                      axis=1, keepdims=True)
        bound.append(qn_ref[0, 0, ch:ch + 1, :] * jnp.sqrt(km2))

    def scores(j, ch, masked):
        ks = pl.multiple_of(j * tk, tk)
        kb = k_ref[0, pl.ds(ks, tk), (ch // 2) * VDIM:(ch // 2 + 1) * VDIM]
        sc = lax.dot_general(kb, qz[ch], (((1,), (1,)), ((), ())),
                             preferred_element_type=F32)
        if masked:
            key = j * tk + lax.broadcasted_iota(jnp.int32, sc.shape, 0)
            qry = qi * tq + lax.broadcasted_iota(jnp.int32, sc.shape, 1)
            sc = jnp.where(key <= qry, sc, NEG_BIG)
        return sc

    def bounded_block(j, masked):
        s = {0: scores(j, 0, masked)}
        for ch in range(n_chain):
            if ch + 1 < n_chain:
                s[ch + 1] = scores(j, ch + 1, masked)
            p = jnp.exp2(s.pop(ch) - bound[ch]).astype(BF16)
            acc_ref[ch] += jnp.dot(vt_ref[0, ch // 2, j], p, preferred_element_type=F32)

    def running_max_block(j, masked):
        s = [scores(j, ch, masked) for ch in range(n_chain)]
        p = []
        alpha = []
        for ch in range(n_chain):
            m_old = m_ref[ch]
            m_new = jnp.maximum(m_old, jnp.max(s[ch], axis=0, keepdims=True))
            alpha.append(jnp.exp2(m_old - m_new))
            p.append(jnp.exp2((s[ch] - m_new).astype(BF16)))
            m_ref[ch] = m_new
        for ch in range(n_chain):
            pv = jnp.dot(vt_ref[0, ch // 2, j], p[ch], preferred_element_type=F32)
            acc_ref[ch] = acc_ref[ch] * alpha[ch] + pv

    def all_blocks(block):
        acc_ref[...] = jnp.zeros(acc_ref.shape, F32)
        n_full = (qi * tq) // tk

        def body(j, carry):
            block(j, False)
            return carry

        lax.fori_loop(0, n_full, body, 0)
        block(n_full, True)

    all_blocks(bounded_block)
    denom_min = jnp.min(jnp.concatenate(
        [acc_ref[ch, VDIM:VDIM + 1, :] for ch in range(n_chain)], axis=0))

    @pl.when(jnp.logical_not(denom_min >= MIN_DENOM))
    def _():
        m_ref[...] = jnp.full(m_ref.shape, NEG_BIG, F32)
        all_blocks(running_max_block)

    lam = _lam(lq1, lk1, lq2, lk2, lam_init)
    for hl in range(HEADS_PER_STEP):
        a1 = acc_ref[2 * hl]
        a2 = acc_ref[2 * hl + 1]
        o = (a1[0:VDIM] / a1[VDIM:VDIM + 1] - lam * (a2[0:VDIM] / a2[VDIM:VDIM + 1]))
        y = o * lax.rsqrt(jnp.mean(o * o, axis=0, keepdims=True) + EPS) * g_ref[...]
        o_ref[0, :, hl * VDIM:(hl + 1) * VDIM] = (y * (1.0 - lam_init)).T.astype(BF16)


def _attention(q, k, vt, qn, kmax, lams, subln_col, lam_init):
    B, S, _ = q.shape
    tq = ATTN_Q_TILE
    tk = vt.shape[4]
    nkb = vt.shape[2]
    hs = HEADS_PER_STEP
    vec = pl.BlockSpec((1, HEAD_DIM), lambda b, h, i: (0, 0))
    return pl.pallas_call(
        functools.partial(_attn_kernel, lam_init=lam_init),
        grid=(B, N_HEADS // hs, S // tq),
        in_specs=[
            vec, vec, vec, vec,
            pl.BlockSpec((VDIM, 1), lambda b, h, i: (0, 0)),
            pl.BlockSpec((1, tq, hs * VDIM), lambda b, h, i: (b, i, h)),
            pl.BlockSpec((1, S, hs * VDIM), lambda b, h, i: (b, 0, h)),
            pl.BlockSpec((1, hs, nkb, VT_ROWS, tk), lambda b, h, i: (b, h, 0, 0, 0)),
            pl.BlockSpec((1, 1, 2 * hs, tq), lambda b, h, i: (b, h, 0, i)),
            pl.BlockSpec((1, 1, VDIM), lambda b, h, i: (b, 0, 0)),
        ],
        out_specs=pl.BlockSpec((1, tq, hs * VDIM), lambda b, h, i: (b, i, h)),
        out_shape=jax.ShapeDtypeStruct((B, S, ATTN_W), BF16),
        scratch_shapes=[
            pltpu.VMEM((2 * hs, 1, tq), F32),
            pltpu.VMEM((2 * hs, VT_ROWS, tq), F32),
        ],
        compiler_params=pltpu.CompilerParams(
            dimension_semantics=("arbitrary", "arbitrary", "arbitrary"),
            vmem_limit_bytes=VMEM_LIMIT),
        name="prompt_attention",
    )(*lams, subln_col, q, k, vt, qn.reshape(B, N_HEADS // hs, 2 * hs, S), kmax)


def _ff_cols(j):
    return slice(j * FF_CHUNK, (j + 1) * FF_CHUNK)


def _ffn_gate(j, a, bg, a2, a1, cw_ref, cb_ref):
    cw = cw_ref[:, _ff_cols(j)]
    c = cw[0:1] * a2 + cw[1:2] * a1 + cw[2:3] * a + cb_ref[:, _ff_cols(j)]
    return ((c * jax.nn.sigmoid(c)) * bg).astype(BF16)


def _ffn_kernel(x_ref, py_ref, ay_ref, wo_ref, g2_ref, wu_ref, cw_ref, cb_ref, wd_ref,
                fg_ref, out_ref, ctail_ref, carry, abuf, bbuf, h2_ref, acc_ref, gbuf, *, final):
    i = pl.program_id(1)
    tm = x_ref.shape[1]

    @pl.when(i == 0)
    def _():
        carry[...] = jnp.zeros(carry.shape, F32)

    x1 = (x_ref[0]
          + jnp.dot(py_ref[0], wo_ref[0:POOL_W, :], preferred_element_type=F32)
          + jnp.dot(ay_ref[0], wo_ref[POOL_W:POOL_W + ATTN_W, :], preferred_element_type=F32))
    h2_ref[...] = _rms(x1, g2_ref[...]).astype(BF16)

    def up(j):
        h2 = h2_ref[...]
        abuf[j % 2, CARRY_ROWS:CARRY_ROWS + tm, :] = jnp.dot(
            h2, wu_ref[:, _ff_cols(j)], preferred_element_type=F32)
        bbuf[j % 2] = jnp.dot(h2, wu_ref[:, _ff_cols(N_FF_CHUNKS + j)],
                              preferred_element_type=F32)

    acc_ref[...] = x1
    up(0)
    for j in range(N_FF_CHUNKS):
        if j + 1 < N_FF_CHUNKS:
            up(j + 1)
        ab = abuf.at[j % 2]
        ab[0:CARRY_ROWS, :] = carry[:, _ff_cols(j)]
        carry[:, _ff_cols(j)] = ab[tm:tm + CARRY_ROWS, :]
        gated = _ffn_gate(j, ab[CARRY_ROWS:CARRY_ROWS + tm, :], bbuf[j % 2],
                          ab[CARRY_ROWS - 2:CARRY_ROWS - 2 + tm, :],
                          ab[CARRY_ROWS - 1:CARRY_ROWS - 1 + tm, :], cw_ref, cb_ref)
        gbuf[:, _ff_cols(j)] = gated

    acc = acc_ref[...] + jnp.dot(gbuf[...], wd_ref[...], preferred_element_type=F32)
    if final:
        acc = _rms(acc, fg_ref[...])
    out_ref[0] = acc
    ctail_ref[0] = carry[...]


def _ffn(x, py, ay, w_out, g2, wu, cw, cb, wd, fg, final):
    B, S, _ = x.shape
    tm = ROW_TILE
    nb = S // tm
    row = lambda b, i: (b, i, 0)
    const2 = lambda b, i: (0, 0)
    resident = dict(pipeline_mode=pl.Buffered(1))
    return pl.pallas_call(
        functools.partial(_ffn_kernel, final=final),
        grid=(B, nb),
        in_specs=[
            pl.BlockSpec((1, tm, D_MODEL), row),
            pl.BlockSpec((1, tm, POOL_W), row),
            pl.BlockSpec((1, tm, ATTN_W), row),
            pl.BlockSpec((D_MODEL, D_MODEL), const2, **resident),
            pl.BlockSpec((1, D_MODEL), const2),
            pl.BlockSpec((D_MODEL, 2 * D_FF), const2, **resident),
            pl.BlockSpec((CONV_W, D_FF), const2),
            pl.BlockSpec((1, D_FF), const2),
            pl.BlockSpec((D_FF, D_MODEL), const2, **resident),
            pl.BlockSpec((1, D_MODEL), const2),
        ],
        out_specs=[
            pl.BlockSpec((1, tm, D_MODEL), row),
            pl.BlockSpec((1, CARRY_ROWS, D_FF), lambda b, i: (b, 0, 0)),
        ],
        out_shape=[
            jax.ShapeDtypeStruct((B, S, D_MODEL), F32),
            jax.ShapeDtypeStruct((B, CARRY_ROWS, D_FF), F32),
        ],
        scratch_shapes=[
            pltpu.VMEM((CARRY_ROWS, D_FF), F32),
            pltpu.VMEM((2, CARRY_ROWS + tm, FF_CHUNK), F32),
            pltpu.VMEM((2, tm, FF_CHUNK), F32),
            pltpu.VMEM((tm, D_MODEL), BF16),
            pltpu.VMEM((tm, D_MODEL), F32),
            pltpu.VMEM((tm, D_FF), BF16),
        ],
        compiler_params=pltpu.CompilerParams(
            dimension_semantics=("arbitrary", "arbitrary"), vmem_limit_bytes=VMEM_LIMIT),
        name="prompt_ffn",
    )(x, py, ay, w_out, g2, wu, cw, cb, wd, fg)


def _s_inproj_kernel(x_ref, g_ref, w_ref, cos_ref, sin_ref, st_ref, pw_ref, ps_ref,
                     q_ref, k_ref, v_ref, u_ref, py_ref, *, pos):
    h = _rms(x_ref[...], g_ref[...]).astype(BF16)
    proj = jnp.dot(h, w_ref[...], preferred_element_type=F32)
    u = proj[:, 0:POOL_W]
    u_ref[...] = u
    cos = cos_ref[...]
    sin = sin_ref[...]
    for hd in range(N_HEADS):
        sl = slice(hd * VDIM, (hd + 1) * VDIM)
        q_ref[:, sl] = _rope(proj[:, POOL_W + hd * VDIM:POOL_W + (hd + 1) * VDIM], cos, sin)
        k_ref[:, sl] = _rope(
            proj[:, POOL_W + ATTN_W + hd * VDIM:POOL_W + ATTN_W + (hd + 1) * VDIM], cos, sin)
    v_ref[...] = proj[:, POOL_W + 2 * ATTN_W:IN_W]
    for g, w in enumerate(POOL_WINDOWS):
        sl = slice(g * POOL_GC, (g + 1) * POOL_GC)
        ug = u[:, sl]
        acc = ug
        for d in range(1, w):
            acc = acc + st_ref[POOL_HIST - d, :, sl]
        cnt = float(min(pos + 1, w))
        dlt = (acc / cnt - ug).astype(BF16)
        y = jnp.dot(dlt, pw_ref[g], preferred_element_type=F32) * ps_ref[:, sl]
        py_ref[:, sl] = y.astype(BF16)


def _s_inproj(x, g1, w_in, cos, sin, state_t, pool_w, pool_scale, pos):
    db = x.shape[0]
    return pl.pallas_call(
        functools.partial(_s_inproj_kernel, pos=pos),
        out_shape=[
            jax.ShapeDtypeStruct((db, ATTN_W), F32),
            jax.ShapeDtypeStruct((db, ATTN_W), F32),
            jax.ShapeDtypeStruct((db, ATTN_W), F32),
            jax.ShapeDtypeStruct((db, POOL_W), F32),
            jax.ShapeDtypeStruct((db, POOL_W), BF16),
        ],
        compiler_params=pltpu.CompilerParams(vmem_limit_bytes=VMEM_LIMIT),
        name="sample_inproj",
    )(x, g1, w_in, cos, sin, state_t, pool_w, pool_scale)


def _s_attn_kernel(pt_ref, lq1, lk1, lq2, lk2, g_ref, q_ref, kn_ref, vn_ref, *rest, lam_init):
    np_ = PAGES_PER_STEP
    k_refs = rest[:np_]
    v_refs = rest[np_:2 * np_]
    o_ref, m_ref, l_ref, acc_ref = rest[2 * np_:]
    j = pl.program_id(1)
    rows = SCORE_ROWS
    prow = k_refs[0].shape[0]

    @pl.when(j == 0)
    def _():
        m_ref[...] = jnp.full(m_ref.shape, NEG_BIG, F32)
        l_ref[...] = jnp.zeros(l_ref.shape, F32)
        acc_ref[...] = jnp.zeros(acc_ref.shape, F32)

    r = lax.broadcasted_iota(jnp.int32, (rows, VDIM), 0)
    lane = lax.broadcasted_iota(jnp.int32, (rows, VDIM), 1)
    qm = jnp.where((lane // HEAD_DIM) == (r % 2), q_ref[0] * Q_SCALE, 0.0).astype(BF16)

    rr = lax.broadcasted_iota(jnp.int32, (rows, prow), 0)
    col = lax.broadcasted_iota(jnp.int32, (rows, prow), 1)
    own = (col % N_HEADS) == (rr // 2)
    s = [jnp.where(own, lax.dot_general(qm, kr[...].astype(BF16), (((1,), (1,)), ((), ())),
                                        preferred_element_type=F32), NEG_BIG)
         for kr in k_refs]
    m_blk = s[0]
    for sp in s[1:]:
        m_blk = jnp.maximum(m_blk, sp)
    m_old = m_ref[...]
    m_new = jnp.maximum(m_old, jnp.max(m_blk, axis=1, keepdims=True))
    alpha = jnp.exp2(m_old - m_new)
    p = [jnp.exp2(sp - m_new).astype(BF16) for sp in s]
    l_blk = p[0].astype(F32)
    for pp in p[1:]:
        l_blk = l_blk + pp.astype(F32)
    l_ref[...] = alpha * l_ref[...] + jnp.sum(l_blk, axis=1, keepdims=True)
    pv = jnp.zeros((rows, VDIM), F32)
    for pp, vr in zip(p, v_refs):
        pv = pv + jnp.dot(pp, vr[...].astype(BF16), preferred_element_type=F32)
    acc_ref[...] = alpha * acc_ref[...] + pv
    m_ref[...] = m_new

    @pl.when(j == pl.num_programs(1) - 1)
    def _():
        kn = kn_ref[0].astype(BF16).astype(F32)
        vn = vn_ref[0].astype(BF16).astype(F32)
        s_new = jnp.sum(qm.astype(F32) * kn, axis=1, keepdims=True)
        m_o = m_ref[...]
        m_n = jnp.maximum(m_o, s_new)
        a_n = jnp.exp2(m_o - m_n)
        p_n = jnp.exp2(s_new - m_n)
        l_n = a_n * l_ref[...] + p_n
        on = (a_n * acc_ref[...] + p_n.astype(BF16).astype(F32) * vn) / l_n
        lam = _lam(lq1, lk1, lq2, lk2, lam_init)
        o = jnp.concatenate(
            [on[2 * hd:2 * hd + 1, :] - lam * on[2 * hd + 1:2 * hd + 2, :]
             for hd in range(N_HEADS)], axis=0)
        y = o * lax.rsqrt(jnp.mean(o * o, axis=-1, keepdims=True) + EPS) * g_ref[...]
        o_ref[0] = y * (1.0 - lam_init)


def _s_attention(page_table, lams, subln_row, q16, kn16, vn16, cache_k, cache_v, layer, lam_init):
    db, n_pages = page_table.shape
    np_ = PAGES_PER_STEP
    prow = cache_k.shape[2]
    rows = SCORE_ROWS
    vec = pl.BlockSpec((1, HEAD_DIM), lambda b, j, pt: (0, 0))
    tok = pl.BlockSpec((1, rows, VDIM), lambda b, j, pt: (b, 0, 0))

    def page_spec(idx):
        return pl.BlockSpec((None, None, prow, VDIM),
                            lambda b, j, pt: (layer, pt[b, j * np_ + idx], 0, 0))

    grid_spec = pltpu.PrefetchScalarGridSpec(
        num_scalar_prefetch=1,
        grid=(db, n_pages // np_),
        in_specs=[vec, vec, vec, vec,
                  pl.BlockSpec((1, VDIM), lambda b, j, pt: (0, 0)),
                  tok, tok, tok]
                 + [page_spec(idx) for idx in range(np_)]
                 + [page_spec(idx) for idx in range(np_)],
        out_specs=pl.BlockSpec((1, N_HEADS, VDIM), lambda b, j, pt: (b, 0, 0)),
        scratch_shapes=[
            pltpu.VMEM((rows, 1), F32),
            pltpu.VMEM((rows, 1), F32),
            pltpu.VMEM((rows, VDIM), F32),
        ],
    )
    return pl.pallas_call(
        functools.partial(_s_attn_kernel, lam_init=lam_init),
        grid_spec=grid_spec,
        out_shape=jax.ShapeDtypeStruct((db, N_HEADS, VDIM), F32),
        compiler_params=pltpu.CompilerParams(
            dimension_semantics=("arbitrary", "arbitrary"), vmem_limit_bytes=VMEM_LIMIT),
        name="sample_attention",
    )(page_table, *lams, subln_row, q16, kn16, vn16, *([cache_k] * np_), *([cache_v] * np_))


def _s_ffn_kernel(x_ref, py_ref, ay_ref, wo_ref, g2_ref, wu_ref, cw_ref, cb_ref, wd_ref,
                  fg_ref, prev_ref, out_ref, a_ref, gbuf, *, final):
    x1 = (x_ref[...]
          + jnp.dot(py_ref[...], wo_ref[0:POOL_W, :], preferred_element_type=F32)
          + jnp.dot(ay_ref[...].astype(BF16), wo_ref[POOL_W:POOL_W + ATTN_W, :],
                    preferred_element_type=F32))
    h2 = _rms(x1, g2_ref[...]).astype(BF16)
    for j in range(N_FF_CHUNKS):
        a = jnp.dot(h2, wu_ref[:, _ff_cols(j)], preferred_element_type=F32)
        bg = jnp.dot(h2, wu_ref[:, _ff_cols(N_FF_CHUNKS + j)], preferred_element_type=F32)
        a_ref[:, _ff_cols(j)] = a
        gbuf[:, _ff_cols(j)] = _ffn_gate(j, a, bg, prev_ref[0, :, _ff_cols(j)],
                                         prev_ref[1, :, _ff_cols(j)], cw_ref, cb_ref)

    out = x1 + jnp.dot(gbuf[...], wd_ref[...], preferred_element_type=F32)
    if final:
        out = _rms(out, fg_ref[...])
    out_ref[...] = out


def _s_ffn(x, py, ay, w_out, g2, wu, cw, cb, wd, fg, prev, final):
    db = x.shape[0]
    return pl.pallas_call(
        functools.partial(_s_ffn_kernel, final=final),
        out_shape=[
            jax.ShapeDtypeStruct((db, D_MODEL), F32),
            jax.ShapeDtypeStruct((db, D_FF), F32),
        ],
        scratch_shapes=[pltpu.VMEM((db, D_FF), BF16)],
        compiler_params=pltpu.CompilerParams(vmem_limit_bytes=VMEM_LIMIT),
        name="sample_ffn",
    )(x, py, ay, w_out, g2, wu, cw, cb, wd, fg, prev)


def _rope_tables(pos):
    half = HEAD_DIM // 2
    inv = 1.0 / (ROPE_THETA ** (jnp.arange(half, dtype=F32) * (2.0 / HEAD_DIM)))
    ang = pos.astype(F32)[:, None] * inv[None, :]
    cos = jnp.cos(ang)
    sin = jnp.sin(ang)
    return (jnp.concatenate([cos, cos, cos, cos], axis=1),
            jnp.concatenate([-sin, sin, -sin, sin], axis=1))


def _score_rows(x):
    db = x.shape[0]
    x = jnp.repeat(x.reshape(db, N_HEADS, VDIM), 2, axis=1)
    return jnp.pad(x, ((0, 0), (0, SCORE_ROWS - 2 * N_HEADS), (0, 0)))


def kernel(x_prompt, x_sample, cache_k, cache_v, state_pool, state_conv, page_table, norm1_g, w_in, pool_w, pool_scale, lam_q1, lam_k1, lam_q2, lam_k2, subln_g, w_out, norm2_g, w_up, conv_w, conv_b, w_down, final_g):
    B, S, _ = x_prompt.shape
    DB, T, _ = x_sample.shape
    depth = w_in.shape[0]
    n_pool, page = cache_k.shape[1], cache_k.shape[2]
    past = page_table.shape[1] * page
    assert T == 1 and S % ROW_TILE == 0 and page_table.shape[1] % PAGES_PER_STEP == 0

    cos_p, sin_p = _rope_tables(jnp.arange(S, dtype=jnp.int32))
    cos_s, sin_s = _rope_tables(past + jnp.arange(T, dtype=jnp.int32))
    ck = cache_k.reshape(depth, n_pool, page * N_HEADS, VDIM)
    cv = cache_v.reshape(depth, n_pool, page * N_HEADS, VDIM)
    fg = final_g.reshape(1, D_MODEL)

    xp = x_prompt
    xs = x_sample.reshape(DB, D_MODEL)
    cache_rows = None
    outs = [[] for _ in range(6)]
    for l in range(depth):
        li = _lambda_init(l)
        final = l == depth - 1
        g1 = norm1_g[l].reshape(1, D_MODEL)
        g2 = norm2_g[l].reshape(1, D_MODEL)
        w_in_b = w_in[l].astype(BF16)
        pool_w_b = pool_w[l].astype(BF16)
        ps = pool_scale[l].reshape(1, POOL_W)
        lams = [v[l].reshape(1, HEAD_DIM) for v in (lam_q1, lam_k1, lam_q2, lam_k2)]
        w_out_b = w_out[l].astype(BF16)
        wu = w_up[l].astype(BF16)
        cw = conv_w[l]
        cb = conv_b[l].reshape(1, D_FF)
        wd = w_down[l].astype(BF16)

        q, k, vt, kf, vf, py, tail, qn, kmax = _inproj(xp, g1, w_in_b, cos_p, sin_p, pool_w_b,
                                                       ps, l, depth, cache_rows)
        cache_rows = (kf, vf)
        ay = _attention(q, k, vt, qn, kmax, lams, subln_g[l].reshape(VDIM, 1), li)
        xp, ctail = _ffn(xp, py, ay, w_out_b, g2, wu, cw, cb, wd, fg, final)
        outs[0].append(tail[:, POOL_CARRY - POOL_HIST:])
        outs[1].append(ctail[:, CARRY_ROWS - (CONV_W - 1):, :])

        state_t = state_pool[l].transpose(1, 0, 2)
        qs, ksn, vsn, us, pys = _s_inproj(xs, g1, w_in_b, cos_s, sin_s, state_t, pool_w_b, ps, past)
        ays = _s_attention(page_table, lams, subln_g[l].reshape(1, VDIM),
                           _score_rows(qs), _score_rows(ksn), _score_rows(vsn), ck, cv, l, li)
        prev = state_conv[l].transpose(1, 0, 2)
        xs, a_s = _s_ffn(xs, pys, ays.reshape(DB, ATTN_W), w_out_b, g2, wu, cw, cb, wd, fg,
                         prev, final)
        a_rows = a_s[:, None, :]
        outs[2].append(ksn.reshape(DB, T, N_HEADS, VDIM))
        outs[3].append(vsn.reshape(DB, T, N_HEADS, VDIM))
        outs[4].append(jnp.concatenate([state_pool[l][:, 1:], us[:, None, :]], axis=1))
        outs[5].append(jnp.concatenate([state_conv[l][:, 1:], a_rows], axis=1))

    kf, vf = cache_rows
    k_new_prompt = kf.reshape(depth, B, S // page, page, N_HEADS, VDIM)
    v_new_prompt = vf.reshape(depth, B, S // page, page, N_HEADS, VDIM)
    y_prompt = xp
    y_sample = xs.reshape(DB, T, D_MODEL)
    stk = [jnp.stack(o) for o in outs]
    return (y_prompt, y_sample, k_new_prompt, v_new_prompt, stk[0], stk[1],
            stk[2], stk[3], stk[4], stk[5])
```

```python
import functools
import math

import jax
import jax.numpy as jnp
from jax import lax
from jax.experimental import pallas as pl
from jax.experimental.pallas import tpu as pltpu

F32 = jnp.float32
BF16 = jnp.bfloat16

D_MODEL = 1024
POOL_W = 512
POOL_WINDOWS = (2, 4, 8, 16)
POOL_GC = 128
POOL_HIST = 15
ATTN_W = 512
HEAD_DIM = 64
VDIM = 128
N_HEADS = 4
IN_W = POOL_W + 3 * ATTN_W
D_FF = 2816
CONV_W = 3
ROPE_THETA = 10000.0
EPS = 1e-6
Q_SCALE = HEAD_DIM ** -0.5 * math.log2(math.e)
NEG_BIG = -1e30
MIN_DENOM = 2.0 ** -100

ROW_TILE = 512
FF_CHUNK = 256
N_FF_CHUNKS = D_FF // FF_CHUNK
CARRY_ROWS = 8
POOL_CARRY = 16
HEADS_PER_STEP = 4
ATTN_Q_TILE = 512
PAGES_PER_STEP = 32
SCORE_ROWS = 16
VT_ROWS = VDIM + 16
VMEM_LIMIT = 56 * 1024 * 1024


def _lambda_init(layer):
    return 0.8 - 0.6 * math.exp(-0.3 * layer)


def _rms(x, g):
    return x * lax.rsqrt(jnp.mean(x * x, axis=-1, keepdims=True) + EPS) * g


def _rope(x, cos, sin_signed):
    lane = lax.broadcasted_iota(jnp.int32, x.shape, 1)
    rot = jnp.where((lane % HEAD_DIM) < HEAD_DIM // 2,
                    pltpu.roll(x, VDIM - HEAD_DIM // 2, 1), pltpu.roll(x, HEAD_DIM // 2, 1))
    return x * cos + rot * sin_signed


def _lam(lq1, lk1, lq2, lk2, lam_init):
    return (jnp.exp(jnp.sum(lq1[...] * lk1[...], axis=-1, keepdims=True))
            - jnp.exp(jnp.sum(lq2[...] * lk2[...], axis=-1, keepdims=True)) + lam_init)


def _inproj_kernel(x_ref, g_ref, w_ref, cos_ref, sin_ref, pw_ref, ps_ref, *rest):
    q_ref, k_ref, vt_ref, kf_ref, vf_ref, py_ref, tail_ref, qn_ref, kmax_ref, ubuf = rest[-10:]
    i = pl.program_id(1)
    tm = x_ref.shape[1]

    @pl.when(i == 0)
    def _():
        ubuf[0:POOL_CARRY, :] = jnp.zeros((POOL_CARRY, POOL_W), F32)

    h = _rms(x_ref[0], g_ref[...]).astype(BF16)
    cos = cos_ref[...]
    sin = sin_ref[...]

    u = jnp.dot(h, w_ref[:, 0:POOL_W], preferred_element_type=F32)
    ubuf[POOL_CARRY:POOL_CARRY + tm, :] = u

    q = jnp.dot(h, w_ref[:, POOL_W:POOL_W + ATTN_W], preferred_element_type=F32)
    k = jnp.dot(h, w_ref[:, POOL_W + ATTN_W:POOL_W + 2 * ATTN_W], preferred_element_type=F32)

    pos = i * tm + lax.broadcasted_iota(jnp.int32, (tm, 1), 0)
    for g, w in enumerate(POOL_WINDOWS):
        sl = slice(g * POOL_GC, (g + 1) * POOL_GC)
        ug = u[:, sl]
        acc = ug
        for d in range(1, w):
            acc = acc + ubuf[POOL_CARRY - d:POOL_CARRY - d + tm, sl]
        cnt = jnp.minimum(pos + 1, w).astype(F32)
        dlt = (acc / cnt - ug).astype(BF16)
        y = jnp.dot(dlt, pw_ref[g], preferred_element_type=F32) * ps_ref[:, sl]
        py_ref[0, :, sl] = y.astype(BF16)

    v = jnp.dot(h, w_ref[:, POOL_W + 2 * ATTN_W:IN_W], preferred_element_type=F32)
    for hd in range(N_HEADS):
        sl = slice(hd * VDIM, (hd + 1) * VDIM)
        q_ref[0, :, sl] = (_rope(q[:, sl], cos, sin) * Q_SCALE).astype(BF16)
        kh = _rope(k[:, sl], cos, sin)
        k_ref[0, :, sl] = kh.astype(BF16)
        vh = v[:, sl]
        vt_ref[0, hd, 0, 0:VDIM, :] = vh.T.astype(BF16)
        vt_ref[0, hd, 0, VDIM:VT_ROWS, :] = jnp.ones((VT_ROWS - VDIM, tm), BF16)
        kf_ref[pl.ds(hd, tm, stride=N_HEADS), :] = kh
        vf_ref[pl.ds(hd, tm, stride=N_HEADS), :] = vh

    row = lax.broadcasted_iota(jnp.int32, (ATTN_W, VDIM), 0)
    col = lax.broadcasted_iota(jnp.int32, (ATTN_W, VDIM), 1)
    sel = jnp.where((row // HEAD_DIM) == col, 1.0, 0.0).astype(BF16)

    def chain_norms(ref):
        sq = jnp.square(ref[0].astype(F32)).astype(BF16)
        return jnp.dot(sq, sel, preferred_element_type=F32)

    qn_ref[0] = jnp.sqrt(chain_norms(q_ref).T[0:2 * N_HEADS, :])
    kmx = jnp.max(chain_norms(k_ref), axis=0, keepdims=True)

    @pl.when(i == 0)
    def _():
        kmax_ref[0] = kmx

    @pl.when(i > 0)
    def _():
        kmax_ref[0] = jnp.maximum(kmax_ref[0], kmx)

    tail = ubuf[tm:tm + POOL_CARRY, :]
    tail_ref[0] = tail
    ubuf[0:POOL_CARRY, :] = tail


def _inproj(x, g1, w_in, cos, sin, pool_w, pool_scale, layer, depth, cache_rows):
    B, S, _ = x.shape
    tm = ROW_TILE
    nb = S // tm
    row = lambda b, i: (b, i, 0)
    const2 = lambda b, i: (0, 0)
    cache_spec = pl.BlockSpec((None, None, N_HEADS * tm, VDIM), lambda b, i: (layer, b, i, 0))
    cache_shape = jax.ShapeDtypeStruct((depth, B, N_HEADS * S, VDIM), F32)
    in_specs = [
        pl.BlockSpec((1, tm, D_MODEL), row),
        pl.BlockSpec((1, D_MODEL), const2),
        pl.BlockSpec((D_MODEL, IN_W), const2),
        pl.BlockSpec((tm, VDIM), lambda b, i: (i, 0)),
        pl.BlockSpec((tm, VDIM), lambda b, i: (i, 0)),
        pl.BlockSpec((len(POOL_WINDOWS), POOL_GC, POOL_GC), lambda b, i: (0, 0, 0)),
        pl.BlockSpec((1, POOL_W), const2),
    ]
    args = [x, g1, w_in, cos, sin, pool_w, pool_scale]
    aliases = {}
    if cache_rows is not None:
        in_specs += [pl.BlockSpec(memory_space=pl.ANY)] * 2
        aliases = {len(args): 3, len(args) + 1: 4}
        args += list(cache_rows)
    return pl.pallas_call(
        _inproj_kernel,
        grid=(B, nb),
        in_specs=in_specs,
        out_specs=[
            pl.BlockSpec((1, tm, ATTN_W), row),
            pl.BlockSpec((1, tm, ATTN_W), row),
            pl.BlockSpec((1, N_HEADS, 1, VT_ROWS, tm), lambda b, i: (b, 0, i, 0, 0)),
            cache_spec,
            cache_spec,
            pl.BlockSpec((1, tm, POOL_W), row),
            pl.BlockSpec((1, POOL_CARRY, POOL_W), lambda b, i: (b, 0, 0)),
            pl.BlockSpec((1, 2 * N_HEADS, tm), lambda b, i: (b, 0, i)),
            pl.BlockSpec((1, 1, VDIM), lambda b, i: (b, 0, 0)),
        ],
        out_shape=[
            jax.ShapeDtypeStruct((B, S, ATTN_W), BF16),
            jax.ShapeDtypeStruct((B, S, ATTN_W), BF16),
            jax.ShapeDtypeStruct((B, N_HEADS, nb, VT_ROWS, tm), BF16),
            cache_shape,
            cache_shape,
            jax.ShapeDtypeStruct((B, S, POOL_W), BF16),
            jax.ShapeDtypeStruct((B, POOL_CARRY, POOL_W), F32),
            jax.ShapeDtypeStruct((B, 2 * N_HEADS, S), F32),
            jax.ShapeDtypeStruct((B, 1, VDIM), F32),
        ],
        input_output_aliases=aliases,
        scratch_shapes=[pltpu.VMEM((POOL_CARRY + tm, POOL_W), F32)],
        compiler_params=pltpu.CompilerParams(
            dimension_semantics=("arbitrary", "arbitrary"), vmem_limit_bytes=VMEM_LIMIT),
        name="prompt_inproj",
    )(*args)


def _attn_kernel(lq1, lk1, lq2, lk2, g_ref, q_ref, k_ref, vt_ref, qn_ref, kmax_ref, o_ref,
                 m_ref, acc_ref, *, lam_init):
    hp = pl.program_id(1)
    qi = pl.program_id(2)
    tq = q_ref.shape[1]
    tk = vt_ref.shape[4]
    n_chain = 2 * HEADS_PER_STEP
    qz = []
    for hl in range(HEADS_PER_STEP):
        q = q_ref[0, :, hl * VDIM:(hl + 1) * VDIM].astype(F32)
        lane = lax.broadcasted_iota(jnp.int32, q.shape, 1)
        qz.append(jnp.where(lane < HEAD_DIM, q, 0.0).astype(BF16))
        qz.append(jnp.where(lane >= HEAD_DIM, q, 0.0).astype(BF16))

    klane = lax.broadcasted_iota(jnp.int32, (1, VDIM), 1)
    bound = []
    for ch in range(n_chain):
        km2 = jnp.max(jnp.where(klane == n_chain * hp + ch, kmax_ref[0], 0.0),
                      axis=1, keepdims=True)
        bound.append(qn_ref[0, 0, ch:ch + 1, :] * jnp.sqrt(km2))

    def scores(j, ch, masked):
        ks = pl.multiple_of(j * tk, tk)
        kb = k_ref[0, pl.ds(ks, tk), (ch // 2) * VDIM:(ch // 2 + 1) * VDIM]
        sc = lax.dot_general(kb, qz[ch], (((1,), (1,)), ((), ())),
                             preferred_element_type=F32)
        if masked:
            key = j * tk + lax.broadcasted_iota(jnp.int32, sc.shape, 0)
            qry = qi * tq + lax.broadcasted_iota(jnp.int32, sc.shape, 1)
            sc = jnp.where(key <= qry, sc, NEG_BIG)
        return sc

    def bounded_block(j, masked):
        s = {0: scores(j, 0, masked)}
        for ch in range(n_chain):
            if ch + 1 < n_chain:
                s[ch + 1] = scores(j, ch + 1, masked)
            p = jnp.exp2(s.pop(ch) - bound[ch]).astype(BF16)
            acc_ref[ch] += jnp.dot(vt_ref[0, ch // 2, j], p, preferred_element_type=F32)

    def running_max_block(j, masked):
        s = [scores(j, ch, masked) for ch in range(n_chain)]
        p = []
        alpha = []
        for ch in range(n_chain):
            m_old = m_ref[ch]
            m_new = jnp.maximum(m_old, jnp.max(s[ch], axis=0, keepdims=True))
            alpha.append(jnp.exp2(m_old - m_new))
            p.append(jnp.exp2((s[ch] - m_new).astype(BF16)))
            m_ref[ch] = m_new
        for ch in range(n_chain):
            pv = jnp.dot(vt_ref[0, ch // 2, j], p[ch], preferred_element_type=F32)
            acc_ref[ch] = acc_ref[ch] * alpha[ch] + pv

    def all_blocks(block):
        acc_ref[...] = jnp.zeros(acc_ref.shape, F32)
        n_full = (qi * tq) // tk

        def body(j, carry):
            block(j, False)
            return carry

        lax.fori_loop(0, n_full, body, 0)
        block(n_full, True)

    all_blocks(bounded_block)
    denom_min = jnp.min(jnp.concatenate(
        [acc_ref[ch, VDIM:VDIM + 1, :] for ch in range(n_chain)], axis=0))

    @pl.when(jnp.logical_not(denom_min >= MIN_DENOM))
    def _():
        m_ref[...] = jnp.full(m_ref.shape, NEG_BIG, F32)
        all_blocks(running_max_block)

    lam = _lam(lq1, lk1, lq2, lk2, lam_init)
    for hl in range(HEADS_PER_STEP):
        a1 = acc_ref[2 * hl]
        a2 = acc_ref[2 * hl + 1]
        o = (a1[0:VDIM] / a1[VDIM:VDIM + 1] - lam * (a2[0:VDIM] / a2[VDIM:VDIM + 1]))
        y = o * lax.rsqrt(jnp.mean(o * o, axis=0, keepdims=True) + EPS) * g_ref[...]
        o_ref[0, :, hl * VDIM:(hl + 1) * VDIM] = (y * (1.0 - lam_init)).T.astype(BF16)


def _attention(q, k, vt, qn, kmax, lams, subln_col, lam_init):
    B, S, _ = q.shape
    tq = ATTN_Q_TILE
    tk = vt.shape[4]
    nkb = vt.shape[2]
    hs = HEADS_PER_STEP
    vec = pl.BlockSpec((1, HEAD_DIM), lambda b, h, i: (0, 0))
    return pl.pallas_call(
        functools.partial(_attn_kernel, lam_init=lam_init),
        grid=(B, N_HEADS // hs, S // tq),
        in_specs=[
            vec, vec, vec, vec,
            pl.BlockSpec((VDIM, 1), lambda b, h, i: (0, 0)),
            pl.BlockSpec((1, tq, hs * VDIM), lambda b, h, i: (b, i, h)),
            pl.BlockSpec((1, S, hs * VDIM), lambda b, h, i: (b, 0, h)),
            pl.BlockSpec((1, hs, nkb, VT_ROWS, tk), lambda b, h, i: (b, h, 0, 0, 0)),
            pl.BlockSpec((1, 1, 2 * hs, tq), lambda b, h, i: (b, h, 0, i)),
            pl.BlockSpec((1, 1, VDIM), lambda b, h, i: (b, 0, 0)),
        ],
        out_specs=pl.BlockSpec((1, tq, hs * VDIM), lambda b, h, i: (b, i, h)),
        out_shape=jax.ShapeDtypeStruct((B, S, ATTN_W), BF16),
        scratch_shapes=[
            pltpu.VMEM((2 * hs, 1, tq), F32),
            pltpu.VMEM((2 * hs, VT_ROWS, tq), F32),
        ],
        compiler_params=pltpu.CompilerParams(
            dimension_semantics=("arbitrary", "arbitrary", "arbitrary"),
            vmem_limit_bytes=VMEM_LIMIT),
        name="prompt_attention",
    )(*lams, subln_col, q, k, vt, qn.reshape(B, N_HEADS // hs, 2 * hs, S), kmax)


def _ff_cols(j):
    return slice(j * FF_CHUNK, (j + 1) * FF_CHUNK)


def _ffn_gate(j, a, bg, a2, a1, cw_ref, cb_ref):
    cw = cw_ref[:, _ff_cols(j)]
    c = cw[0:1] * a2 + cw[1:2] * a1 + cw[2:3] * a + cb_ref[:, _ff_cols(j)]
    return ((c * jax.nn.sigmoid(c)) * bg).astype(BF16)


def _ffn_kernel(x_ref, py_ref, ay_ref, wo_ref, g2_ref, wu_ref, cw_ref, cb_ref, wd_ref,
                fg_ref, out_ref, ctail_ref, carry, abuf, bbuf, h2_ref, acc_ref, gbuf, *, final):
    i = pl.program_id(1)
    tm = x_ref.shape[1]

    @pl.when(i == 0)
    def _():
        carry[...] = jnp.zeros(carry.shape, F32)

    x1 = (x_ref[0]
          + jnp.dot(py_ref[0], wo_ref[0:POOL_W, :], preferred_element_type=F32)
          + jnp.dot(ay_ref[0], wo_ref[POOL_W:POOL_W + ATTN_W, :], preferred_element_type=F32))
    h2_ref[...] = _rms(x1, g2_ref[...]).astype(BF16)

    def up(j):
        h2 = h2_ref[...]
        abuf[j % 2, CARRY_ROWS:CARRY_ROWS + tm, :] = jnp.dot(
            h2, wu_ref[:, _ff_cols(j)], preferred_element_type=F32)
        bbuf[j % 2] = jnp.dot(h2, wu_ref[:, _ff_cols(N_FF_CHUNKS + j)],
                              preferred_element_type=F32)

    acc_ref[...] = x1
    up(0)
    for j in range(N_FF_CHUNKS):
        if j + 1 < N_FF_CHUNKS:
            up(j + 1)
        ab = abuf.at[j % 2]
        ab[0:CARRY_ROWS, :] = carry[:, _ff_cols(j)]
        carry[:, _ff_cols(j)] = ab[tm:tm + CARRY_ROWS, :]
        gated = _ffn_gate(j, ab[CARRY_ROWS:CARRY_ROWS + tm, :], bbuf[j % 2],
                          ab[CARRY_ROWS - 2:CARRY_ROWS - 2 + tm, :],
                          ab[CARRY_ROWS - 1:CARRY_ROWS - 1 + tm, :], cw_ref, cb_ref)
        gbuf[:, _ff_cols(j)] = gated

    acc = acc_ref[...] + jnp.dot(gbuf[...], wd_ref[...], preferred_element_type=F32)
    if final:
        acc = _rms(acc, fg_ref[...])
    out_ref[0] = acc
    ctail_ref[0] = carry[...]


def _ffn(x, py, ay, w_out, g2, wu, cw, cb, wd, fg, final):
    B, S, _ = x.shape
    tm = ROW_TILE
    nb = S // tm
    row = lambda b, i: (b, i, 0)
    const2 = lambda b, i: (0, 0)
    resident = dict(pipeline_mode=pl.Buffered(1))
    return pl.pallas_call(
        functools.partial(_ffn_kernel, final=final),
        grid=(B, nb),
        in_specs=[
            pl.BlockSpec((1, tm, D_MODEL), row),
            pl.BlockSpec((1, tm, POOL_W), row),
            pl.BlockSpec((1, tm, ATTN_W), row),
            pl.BlockSpec((D_MODEL, D_MODEL), const2, **resident),
            pl.BlockSpec((1, D_MODEL), const2),
            pl.BlockSpec((D_MODEL, 2 * D_FF), const2, **resident),
            pl.BlockSpec((CONV_W, D_FF), const2),
            pl.BlockSpec((1, D_FF), const2),
            pl.BlockSpec((D_FF, D_MODEL), const2, **resident),
            pl.BlockSpec((1, D_MODEL), const2),
        ],
        out_specs=[
            pl.BlockSpec((1, tm, D_MODEL), row),
            pl.BlockSpec((1, CARRY_ROWS, D_FF), lambda b, i: (b, 0, 0)),
        ],
        out_shape=[
            jax.ShapeDtypeStruct((B, S, D_MODEL), F32),
            jax.ShapeDtypeStruct((B, CARRY_ROWS, D_FF), F32),
        ],
        scratch_shapes=[
            pltpu.VMEM((CARRY_ROWS, D_FF), F32),
            pltpu.VMEM((2, CARRY_ROWS + tm, FF_CHUNK), F32),
            pltpu.VMEM((2, tm, FF_CHUNK), F32),
            pltpu.VMEM((tm, D_MODEL), BF16),
            pltpu.VMEM((tm, D_MODEL), F32),
            pltpu.VMEM((tm, D_FF), BF16),
        ],
        compiler_params=pltpu.CompilerParams(
            dimension_semantics=("arbitrary", "arbitrary"), vmem_limit_bytes=VMEM_LIMIT),
        name="prompt_ffn",
    )(x, py, ay, w_out, g2, wu, cw, cb, wd, fg)


def _s_inproj_kernel(x_ref, g_ref, w_ref, cos_ref, sin_ref, st_ref, pw_ref, ps_ref,
                     q_ref, k_ref, v_ref, u_ref, py_ref, *, pos):
    h = _rms(x_ref[...], g_ref[...]).astype(BF16)
    proj = jnp.dot(h, w_ref[...], preferred_element_type=F32)
    u = proj[:, 0:POOL_W]
    u_ref[...] = u
    cos = cos_ref[...]
    sin = sin_ref[...]
    for hd in range(N_HEADS):
        sl = slice(hd * VDIM, (hd + 1) * VDIM)
        q_ref[:, sl] = _rope(proj[:, POOL_W + hd * VDIM:POOL_W + (hd + 1) * VDIM], cos, sin)
        k_ref[:, sl] = _rope(
            proj[:, POOL_W + ATTN_W + hd * VDIM:POOL_W + ATTN_W + (hd + 1) * VDIM], cos, sin)
    v_ref[...] = proj[:, POOL_W + 2 * ATTN_W:IN_W]
    for g, w in enumerate(POOL_WINDOWS):
        sl = slice(g * POOL_GC, (g + 1) * POOL_GC)
        ug = u[:, sl]
        acc = ug
        for d in range(1, w):
            acc = acc + st_ref[POOL_HIST - d, :, sl]
        cnt = float(min(pos + 1, w))
        dlt = (acc / cnt - ug).astype(BF16)
        y = jnp.dot(dlt, pw_ref[g], preferred_element_type=F32) * ps_ref[:, sl]
        py_ref[:, sl] = y.astype(BF16)


def _s_inproj(x, g1, w_in, cos, sin, state_t, pool_w, pool_scale, pos):
    db = x.shape[0]
    return pl.pallas_call(
        functools.partial(_s_inproj_kernel, pos=pos),
        out_shape=[
            jax.ShapeDtypeStruct((db, ATTN_W), F32),
            jax.ShapeDtypeStruct((db, ATTN_W), F32),
            jax.ShapeDtypeStruct((db, ATTN_W), F32),
            jax.ShapeDtypeStruct((db, POOL_W), F32),
            jax.ShapeDtypeStruct((db, POOL_W), BF16),
        ],
        compiler_params=pltpu.CompilerParams(vmem_limit_bytes=VMEM_LIMIT),
        name="sample_inproj",
    )(x, g1, w_in, cos, sin, state_t, pool_w, pool_scale)


def _s_attn_kernel(pt_ref, lq1, lk1, lq2, lk2, g_ref, q_ref, kn_ref, vn_ref, *rest, lam_init):
    np_ = PAGES_PER_STEP
    k_refs = rest[:np_]
    v_refs = rest[np_:2 * np_]
    o_ref, m_ref, l_ref, acc_ref = rest[2 * np_:]
    j = pl.program_id(1)
    rows = SCORE_ROWS
    prow = k_refs[0].shape[0]

    @pl.when(j == 0)
    def _():
        m_ref[...] = jnp.full(m_ref.shape, NEG_BIG, F32)
        l_ref[...] = jnp.zeros(l_ref.shape, F32)
        acc_ref[...] = jnp.zeros(acc_ref.shape, F32)

    r = lax.broadcasted_iota(jnp.int32, (rows, VDIM), 0)
    lane = lax.broadcasted_iota(jnp.int32, (rows, VDIM), 1)
    qm = jnp.where((lane // HEAD_DIM) == (r % 2), q_ref[0] * Q_SCALE, 0.0).astype(BF16)

    rr = lax.broadcasted_iota(jnp.int32, (rows, prow), 0)
    col = lax.broadcasted_iota(jnp.int32, (rows, prow), 1)
    own = (col % N_HEADS) == (rr // 2)
    s = [jnp.where(own, lax.dot_general(qm, kr[...].astype(BF16), (((1,), (1,)), ((), ())),
                                        preferred_element_type=F32), NEG_BIG)
         for kr in k_refs]
    m_blk = s[0]
    for sp in s[1:]:
        m_blk = jnp.maximum(m_blk, sp)
    m_old = m_ref[...]
    m_new = jnp.maximum(m_old, jnp.max(m_blk, axis=1, keepdims=True))
    alpha = jnp.exp2(m_old - m_new)
    p = [jnp.exp2(sp - m_new).astype(BF16) for sp in s]
    l_blk = p[0].astype(F32)
    for pp in p[1:]:
        l_blk = l_blk + pp.astype(F32)
    l_ref[...] = alpha * l_ref[...] + jnp.sum(l_blk, axis=1, keepdims=True)
    pv = jnp.zeros((rows, VDIM), F32)
    for pp, vr in zip(p, v_refs):
        pv = pv + jnp.dot(pp, vr[...].astype(BF16), preferred_element_type=F32)
    acc_ref[...] = alpha * acc_ref[...] + pv
    m_ref[...] = m_new

    @pl.when(j == pl.num_programs(1) - 1)
    def _():
        kn = kn_ref[0].astype(BF16).astype(F32)
        vn = vn_ref[0].astype(BF16).astype(F32)
        s_new = jnp.sum(qm.astype(F32) * kn, axis=1, keepdims=True)
        m_o = m_ref[...]
        m_n = jnp.maximum(m_o, s_new)
        a_n = jnp.exp2(m_o - m_n)
        p_n = jnp.exp2(s_new - m_n)
        l_n = a_n * l_ref[...] + p_n
        on = (a_n * acc_ref[...] + p_n.astype(BF16).astype(F32) * vn) / l_n
        lam = _lam(lq1, lk1, lq2, lk2, lam_init)
        o = jnp.concatenate(
            [on[2 * hd:2 * hd + 1, :] - lam * on[2 * hd + 1:2 * hd + 2, :]
             for hd in range(N_HEADS)], axis=0)
        y = o * lax.rsqrt(jnp.mean(o * o, axis=-1, keepdims=True) + EPS) * g_ref[...]
        o_ref[0] = y * (1.0 - lam_init)


def _s_attention(page_table, lams, subln_row, q16, kn16, vn16, cache_k, cache_v, layer, lam_init):
    db, n_pages = page_table.shape
    np_ = PAGES_PER_STEP
    prow = cache_k.shape[2]
    rows = SCORE_ROWS
    vec = pl.BlockSpec((1, HEAD_DIM), lambda b, j, pt: (0, 0))
    tok = pl.BlockSpec((1, rows, VDIM), lambda b, j, pt: (b, 0, 0))

    def page_spec(idx):
        return pl.BlockSpec((None, None, prow, VDIM),
                            lambda b, j, pt: (layer, pt[b, j * np_ + idx], 0, 0))

    grid_spec = pltpu.PrefetchScalarGridSpec(
        num_scalar_prefetch=1,
        grid=(db, n_pages // np_),
        in_specs=[vec, vec, vec, vec,
                  pl.BlockSpec((1, VDIM), lambda b, j, pt: (0, 0)),
                  tok, tok, tok]
                 + [page_spec(idx) for idx in range(np_)]
                 + [page_spec(idx) for idx in range(np_)],
        out_specs=pl.BlockSpec((1, N_HEADS, VDIM), lambda b, j, pt: (b, 0, 0)),
        scratch_shapes=[
            pltpu.VMEM((rows, 1), F32),
            pltpu.VMEM((rows, 1), F32),
            pltpu.VMEM((rows, VDIM), F32),
        ],
    )
    return pl.pallas_call(
        functools.partial(_s_attn_kernel, lam_init=lam_init),
        grid_spec=grid_spec,
        out_shape=jax.ShapeDtypeStruct((db, N_HEADS, VDIM), F32),
        compiler_params=pltpu.CompilerParams(
            dimension_semantics=("arbitrary", "arbitrary"), vmem_limit_bytes=VMEM_LIMIT),
        name="sample_attention",
    )(page_table, *lams, subln_row, q16, kn16, vn16, *([cache_k] * np_), *([cache_v] * np_))


def _s_ffn_kernel(x_ref, py_ref, ay_ref, wo_ref, g2_ref, wu_ref, cw_ref, cb_ref, wd_ref,
                  fg_ref, prev_ref, out_ref, a_ref, gbuf, *, final):
    x1 = (x_ref[...]
          + jnp.dot(py_ref[...], wo_ref[0:POOL_W, :], preferred_element_type=F32)
          + jnp.dot(ay_ref[...].astype(BF16), wo_ref[POOL_W:POOL_W + ATTN_W, :],
                    preferred_element_type=F32))
    h2 = _rms(x1, g2_ref[...]).astype(BF16)
    for j in range(N_FF_CHUNKS):
        a = jnp.dot(h2, wu_ref[:, _ff_cols(j)], preferred_element_type=F32)
        bg = jnp.dot(h2, wu_ref[:, _ff_cols(N_FF_CHUNKS + j)], preferred_element_type=F32)
        a_ref[:, _ff_cols(j)] = a
        gbuf[:, _ff_cols(j)] = _ffn_gate(j, a, bg, prev_ref[0, :, _ff_cols(j)],
                                         prev_ref[1, :, _ff_cols(j)], cw_ref, cb_ref)

    out = x1 + jnp.dot(gbuf[...], wd_ref[...], preferred_element_type=F32)
    if final:
        out = _rms(out, fg_ref[...])
    out_ref[...] = out


def _s_ffn(x, py, ay, w_out, g2, wu, cw, cb, wd, fg, prev, final):
    db = x.shape[0]
    return pl.pallas_call(
        functools.partial(_s_ffn_kernel, final=final),
        out_shape=[
            jax.ShapeDtypeStruct((db, D_MODEL), F32),
            jax.ShapeDtypeStruct((db, D_FF), F32),
        ],
        scratch_shapes=[pltpu.VMEM((db, D_FF), BF16)],
        compiler_params=pltpu.CompilerParams(vmem_limit_bytes=VMEM_LIMIT),
        name="sample_ffn",
    )(x, py, ay, w_out, g2, wu, cw, cb, wd, fg, prev)


def _rope_tables(pos):
    half = HEAD_DIM // 2
    inv = 1.0 / (ROPE_THETA ** (jnp.arange(half, dtype=F32) * (2.0 / HEAD_DIM)))
    ang = pos.astype(F32)[:, None] * inv[None, :]
    cos = jnp.cos(ang)
    sin = jnp.sin(ang)
    return (jnp.concatenate([cos, cos, cos, cos], axis=1),
            jnp.concatenate([-sin, sin, -sin, sin], axis=1))


def _score_rows(x):
    db = x.shape[0]
    x = jnp.repeat(x.reshape(db, N_HEADS, VDIM), 2, axis=1)
    return jnp.pad(x, ((0, 0), (0, SCORE_ROWS - 2 * N_HEADS), (0, 0)))


def kernel(x_prompt, x_sample, cache_k, cache_v, state_pool, state_conv, page_table, norm1_g, w_in, pool_w, pool_scale, lam_q1, lam_k1, lam_q2, lam_k2, subln_g, w_out, norm2_g, w_up, conv_w, conv_b, w_down, final_g):
    B, S, _ = x_prompt.shape
    DB, T, _ = x_sample.shape
    depth = w_in.shape[0]
    n_pool, page = cache_k.shape[1], cache_k.shape[2]
    past = page_table.shape[1] * page
    assert T == 1 and S % ROW_TILE == 0 and page_table.shape[1] % PAGES_PER_STEP == 0

    cos_p, sin_p = _rope_tables(jnp.arange(S, dtype=jnp.int32))
    cos_s, sin_s = _rope_tables(past + jnp.arange(T, dtype=jnp.int32))
    ck = cache_k.reshape(depth, n_pool, page * N_HEADS, VDIM)
    cv = cache_v.reshape(depth, n_pool, page * N_HEADS, VDIM)
    fg = final_g.reshape(1, D_MODEL)

    xp = x_prompt
    xs = x_sample.reshape(DB, D_MODEL)
    cache_rows = None
    outs = [[] for _ in range(6)]
    for l in range(depth):
        li = _lambda_init(l)
        final = l == depth - 1
        g1 = norm1_g[l].reshape(1, D_MODEL)
        g2 = norm2_g[l].reshape(1, D_MODEL)
        w_in_b = w_in[l].astype(BF16)
        pool_w_b = pool_w[l].astype(BF16)
        ps = pool_scale[l].reshape(1, POOL_W)
        lams = [v[l].reshape(1, HEAD_DIM) for v in (lam_q1, lam_k1, lam_q2, lam_k2)]
        w_out_b = w_out[l].astype(BF16)
        wu = w_up[l].astype(BF16)
        cw = conv_w[l]
        cb = conv_b[l].reshape(1, D_FF)
        wd = w_down[l].astype(BF16)

        q, k, vt, kf, vf, py, tail, qn, kmax = _inproj(xp, g1, w_in_b, cos_p, sin_p, pool_w_b,
                                                       ps, l, depth, cache_rows)
        cache_rows = (kf, vf)
        ay = _attention(q, k, vt, qn, kmax, lams, subln_g[l].reshape(VDIM, 1), li)
        xp, ctail = _ffn(xp, py, ay, w_out_b, g2, wu, cw, cb, wd, fg, final)
        outs[0].append(tail[:, POOL_CARRY - POOL_HIST:])
        outs[1].append(ctail[:, CARRY_ROWS - (CONV_W - 1):, :])

        state_t = state_pool[l].transpose(1, 0, 2)
        qs, ksn, vsn, us, pys = _s_inproj(xs, g1, w_in_b, cos_s, sin_s, state_t, pool_w_b, ps, past)
        ays = _s_attention(page_table, lams, subln_g[l].reshape(1, VDIM),
                           _score_rows(qs), _score_rows(ksn), _score_rows(vsn), ck, cv, l, li)
        prev = state_conv[l].transpose(1, 0, 2)
        xs, a_s = _s_ffn(xs, pys, ays.reshape(DB, ATTN_W), w_out_b, g2, wu, cw, cb, wd, fg,
                         prev, final)
        a_rows = a_s[:, None, :]
        outs[2].append(ksn.reshape(DB, T, N_HEADS, VDIM))
        outs[3].append(vsn.reshape(DB, T, N_HEADS, VDIM))
        outs[4].append(jnp.concatenate([state_pool[l][:, 1:], us[:, None, :]], axis=1))
        outs[5].append(jnp.concatenate([state_conv[l][:, 1:], a_rows], axis=1))

    kf, vf = cache_rows
    k_new_prompt = kf.reshape(depth, B, S // page, page, N_HEADS, VDIM)
    v_new_prompt = vf.reshape(depth, B, S // page, page, N_HEADS, VDIM)
    y_prompt = xp
    y_sample = xs.reshape(DB, T, D_MODEL)
    stk = [jnp.stack(o) for o in outs]
    return (y_prompt, y_sample, k_new_prompt, v_new_prompt, stk[0], stk[1],
            stk[2], stk[3], stk[4], stk[5])
```

```python
import functools
import math

import jax
import jax.numpy as jnp
from jax import lax
from jax.experimental import pallas as pl
from jax.experimental.pallas import tpu as pltpu

F32 = jnp.float32
BF16 = jnp.bfloat16

D_MODEL = 1024
POOL_W = 512
POOL_WINDOWS = (2, 4, 8, 16)
POOL_GC = 128
POOL_HIST = 15
ATTN_W = 512
HEAD_DIM = 64
VDIM = 128
N_HEADS = 4
IN_W = POOL_W + 3 * ATTN_W
D_FF = 2816
CONV_W = 3
ROPE_THETA = 10000.0
EPS = 1e-6
Q_SCALE = HEAD_DIM ** -0.5 * math.log2(math.e)
NEG_BIG = -1e30
MIN_DENOM = 2.0 ** -100

ROW_TILE = 512
FF_CHUNK = 256
N_FF_CHUNKS = D_FF // FF_CHUNK
CARRY_ROWS = 8
POOL_CARRY = 16
HEADS_PER_STEP = 4
ATTN_Q_TILE = 512
PAGES_PER_STEP = 16
DECODE_BATCH_PER_STEP = 2
SCORE_ROWS = 16
VT_ROWS = VDIM + 16
VMEM_LIMIT = 56 * 1024 * 1024


def _lambda_init(layer):
    return 0.8 - 0.6 * math.exp(-0.3 * layer)


def _rms(x, g):
    return x * lax.rsqrt(jnp.mean(x * x, axis=-1, keepdims=True) + EPS) * g


def _rope(x, cos, sin_signed):
    lane = lax.broadcasted_iota(jnp.int32, x.shape, 1)
    rot = jnp.where((lane % HEAD_DIM) < HEAD_DIM // 2,
                    pltpu.roll(x, VDIM - HEAD_DIM // 2, 1), pltpu.roll(x, HEAD_DIM // 2, 1))
    return x * cos + rot * sin_signed


def _lam(lq1, lk1, lq2, lk2, lam_init):
    return (jnp.exp(jnp.sum(lq1[...] * lk1[...], axis=-1, keepdims=True))
            - jnp.exp(jnp.sum(lq2[...] * lk2[...], axis=-1, keepdims=True)) + lam_init)


def _inproj_kernel(x_ref, g_ref, w_ref, cos_ref, sin_ref, pw_ref, ps_ref, *rest):
    q_ref, k_ref, vt_ref, kf_ref, vf_ref, py_ref, tail_ref, qn_ref, kmax_ref, ubuf = rest[-10:]
    i = pl.program_id(1)
    tm = x_ref.shape[1]

    @pl.when(i == 0)
    def _():
        ubuf[0:POOL_CARRY, :] = jnp.zeros((POOL_CARRY, POOL_W), F32)

    h = _rms(x_ref[0], g_ref[...]).astype(BF16)
    cos = cos_ref[...]
    sin = sin_ref[...]

    u = jnp.dot(h, w_ref[:, 0:POOL_W], preferred_element_type=F32)
    ubuf[POOL_CARRY:POOL_CARRY + tm, :] = u

    q = jnp.dot(h, w_ref[:, POOL_W:POOL_W + ATTN_W], preferred_element_type=F32)
    k = jnp.dot(h, w_ref[:, POOL_W + ATTN_W:POOL_W + 2 * ATTN_W], preferred_element_type=F32)

    pos = i * tm + lax.broadcasted_iota(jnp.int32, (tm, 1), 0)
    for g, w in enumerate(POOL_WINDOWS):
        sl = slice(g * POOL_GC, (g + 1) * POOL_GC)
        ug = u[:, sl]
        acc = ug
        for d in range(1, w):
            acc = acc + ubuf[POOL_CARRY - d:POOL_CARRY - d + tm, sl]
        cnt = jnp.minimum(pos + 1, w).astype(F32)
        dlt = (acc / cnt - ug).astype(BF16)
        y = jnp.dot(dlt, pw_ref[g], preferred_element_type=F32) * ps_ref[:, sl]
        py_ref[0, :, sl] = y.astype(BF16)

    v = jnp.dot(h, w_ref[:, POOL_W + 2 * ATTN_W:IN_W], preferred_element_type=F32)
    for hd in range(N_HEADS):
        sl = slice(hd * VDIM, (hd + 1) * VDIM)
        q_ref[0, :, sl] = (_rope(q[:, sl], cos, sin) * Q_SCALE).astype(BF16)
        kh = _rope(k[:, sl], cos, sin)
        k_ref[0, :, sl] = kh.astype(BF16)
        vh = v[:, sl]
        vt_ref[0, hd, 0, 0:VDIM, :] = vh.T.astype(BF16)
        vt_ref[0, hd, 0, VDIM:VT_ROWS, :] = jnp.ones((VT_ROWS - VDIM, tm), BF16)
        kf_ref[pl.ds(hd, tm, stride=N_HEADS), :] = kh
        vf_ref[pl.ds(hd, tm, stride=N_HEADS), :] = vh

    row = lax.broadcasted_iota(jnp.int32, (ATTN_W, VDIM), 0)
    col = lax.broadcasted_iota(jnp.int32, (ATTN_W, VDIM), 1)
    sel = jnp.where((row // HEAD_DIM) == col, 1.0, 0.0).astype(BF16)

    def chain_norms(ref):
        sq = jnp.square(ref[0].astype(F32)).astype(BF16)
        return jnp.dot(sq, sel, preferred_element_type=F32)

    qn_ref[0] = jnp.sqrt(chain_norms(q_ref).T[0:2 * N_HEADS, :])
    kmx = jnp.max(chain_norms(k_ref), axis=0, keepdims=True)

    @pl.when(i == 0)
    def _():
        kmax_ref[0] = kmx

    @pl.when(i > 0)
    def _():
        kmax_ref[0] = jnp.maximum(kmax_ref[0], kmx)

    tail = ubuf[tm:tm + POOL_CARRY, :]
    tail_ref[0] = tail
    ubuf[0:POOL_CARRY, :] = tail


def _inproj(x, g1, w_in, cos, sin, pool_w, pool_scale, layer, depth, cache_rows):
    B, S, _ = x.shape
    tm = ROW_TILE
    nb = S // tm
    row = lambda b, i: (b, i, 0)
    const2 = lambda b, i: (0, 0)
    cache_spec = pl.BlockSpec((None, None, N_HEADS * tm, VDIM), lambda b, i: (layer, b, i, 0))
    cache_shape = jax.ShapeDtypeStruct((depth, B, N_HEADS * S, VDIM), F32)
    in_specs = [
        pl.BlockSpec((1, tm, D_MODEL), row),
        pl.BlockSpec((1, D_MODEL), const2),
        pl.BlockSpec((D_MODEL, IN_W), const2),
        pl.BlockSpec((tm, VDIM), lambda b, i: (i, 0)),
        pl.BlockSpec((tm, VDIM), lambda b, i: (i, 0)),
        pl.BlockSpec((len(POOL_WINDOWS), POOL_GC, POOL_GC), lambda b, i: (0, 0, 0)),
        pl.BlockSpec((1, POOL_W), const2),
    ]
    args = [x, g1, w_in, cos, sin, pool_w, pool_scale]
    aliases = {}
    if cache_rows is not None:
        in_specs += [pl.BlockSpec(memory_space=pl.ANY)] * 2
        aliases = {len(args): 3, len(args) + 1: 4}
        args += list(cache_rows)
    return pl.pallas_call(
        _inproj_kernel,
        grid=(B, nb),
        in_specs=in_specs,
        out_specs=[
            pl.BlockSpec((1, tm, ATTN_W), row),
            pl.BlockSpec((1, tm, ATTN_W), row),
            pl.BlockSpec((1, N_HEADS, 1, VT_ROWS, tm), lambda b, i: (b, 0, i, 0, 0)),
            cache_spec,
            cache_spec,
            pl.BlockSpec((1, tm, POOL_W), row),
            pl.BlockSpec((1, POOL_CARRY, POOL_W), lambda b, i: (b, 0, 0)),
            pl.BlockSpec((1, 2 * N_HEADS, tm), lambda b, i: (b, 0, i)),
            pl.BlockSpec((1, 1, VDIM), lambda b, i: (b, 0, 0)),
        ],
        out_shape=[
            jax.ShapeDtypeStruct((B, S, ATTN_W), BF16),
            jax.ShapeDtypeStruct((B, S, ATTN_W), BF16),
            jax.ShapeDtypeStruct((B, N_HEADS, nb, VT_ROWS, tm), BF16),
            cache_shape,
            cache_shape,
            jax.ShapeDtypeStruct((B, S, POOL_W), BF16),
            jax.ShapeDtypeStruct((B, POOL_CARRY, POOL_W), F32),
            jax.ShapeDtypeStruct((B, 2 * N_HEADS, S), F32),
            jax.ShapeDtypeStruct((B, 1, VDIM), F32),
        ],
        input_output_aliases=aliases,
        scratch_shapes=[pltpu.VMEM((POOL_CARRY + tm, POOL_W), F32)],
        compiler_params=pltpu.CompilerParams(
            dimension_semantics=("arbitrary", "arbitrary"), vmem_limit_bytes=VMEM_LIMIT),
        name="prompt_inproj",
    )(*args)


def _attn_kernel(lq1, lk1, lq2, lk2, g_ref, q_ref, k_ref, vt_ref, qn_ref, kmax_ref, o_ref,
                 m_ref, acc_ref, *, lam_init):
    hp = pl.program_id(1)
    qi = pl.program_id(2)
    tq = q_ref.shape[1]
    tk = vt_ref.shape[4]
    n_chain = 2 * HEADS_PER_STEP
    qz = []
    for hl in range(HEADS_PER_STEP):
        q = q_ref[0, :, hl * VDIM:(hl + 1) * VDIM].astype(F32)
        lane = lax.broadcasted_iota(jnp.int32, q.shape, 1)
        qz.append(jnp.where(lane < HEAD_DIM, q, 0.0).astype(BF16))
        qz.append(jnp.where(lane >= HEAD_DIM, q, 0.0).astype(BF16))

    klane = lax.broadcasted_iota(jnp.int32, (1, VDIM), 1)
    bound = []
    for ch in range(n_chain):
        km2 = jnp.max(jnp.where(klane == n_chain * hp + ch, kmax_ref[0], 0.0),
                      axis=1, keepdims=True)
        bound.append(qn_ref[0, 0, ch:ch + 1, :] * jnp.sqrt(km2))

    def scores(j, ch, masked):
        ks = pl.multiple_of(j * tk, tk)
        kb = k_ref[0, pl.ds(ks, tk), (ch // 2) * VDIM:(ch // 2 + 1) * VDIM]
        sc = lax.dot_general(kb, qz[ch], (((1,), (1,)), ((), ())),
                             preferred_element_type=F32)
        if masked:
            key = j * tk + lax.broadcasted_iota(jnp.int32, sc.shape, 0)
            qry = qi * tq + lax.broadcasted_iota(jnp.int32, sc.shape, 1)
            sc = jnp.where(key <= qry, sc, NEG_BIG)
        return sc

    def bounded_block(j, masked):
        s = {0: scores(j, 0, masked)}
        for ch in range(n_chain):
            if ch + 1 < n_chain:
                s[ch + 1] = scores(j, ch + 1, masked)
            p = jnp.exp2(s.pop(ch) - bound[ch]).astype(BF16)
            acc_ref[ch] += jnp.dot(vt_ref[0, ch // 2, j], p, preferred_element_type=F32)

    def running_max_block(j, masked):
        s = [scores(j, ch, masked) for ch in range(n_chain)]
        p = []
        alpha = []
        for ch in range(n_chain):
            m_old = m_ref[ch]
            m_new = jnp.maximum(m_old, jnp.max(s[ch], axis=0, keepdims=True))
            alpha.append(jnp.exp2(m_old - m_new))
            p.append(jnp.exp2((s[ch] - m_new).astype(BF16)))
            m_ref[ch] = m_new
        for ch in range(n_chain):
            pv = jnp.dot(vt_ref[0, ch // 2, j], p[ch], preferred_element_type=F32)
            acc_ref[ch] = acc_ref[ch] * alpha[ch] + pv

    def all_blocks(block):
        acc_ref[...] = jnp.zeros(acc_ref.shape, F32)
        n_full = (qi * tq) // tk

        def body(j, carry):
            block(j, False)
            return carry

        lax.fori_loop(0, n_full, body, 0)
        block(n_full, True)

    all_blocks(bounded_block)
    denom_min = jnp.min(jnp.concatenate(
        [acc_ref[ch, VDIM:VDIM + 1, :] for ch in range(n_chain)], axis=0))

    @pl.when(jnp.logical_not(denom_min >= MIN_DENOM))
    def _():
        m_ref[...] = jnp.full(m_ref.shape, NEG_BIG, F32)
        all_blocks(running_max_block)

    lam = _lam(lq1, lk1, lq2, lk2, lam_init)
    for hl in range(HEADS_PER_STEP):
        a1 = acc_ref[2 * hl]
        a2 = acc_ref[2 * hl + 1]
        o = (a1[0:VDIM] / a1[VDIM:VDIM + 1] - lam * (a2[0:VDIM] / a2[VDIM:VDIM + 1]))
        y = o * lax.rsqrt(jnp.mean(o * o, axis=0, keepdims=True) + EPS) * g_ref[...]
        o_ref[0, :, hl * VDIM:(hl + 1) * VDIM] = (y * (1.0 - lam_init)).T.astype(BF16)


def _attention(q, k, vt, qn, kmax, lams, subln_col, lam_init):
    B, S, _ = q.shape
    tq = ATTN_Q_TILE
    tk = vt.shape[4]
    nkb = vt.shape[2]
    hs = HEADS_PER_STEP
    vec = pl.BlockSpec((1, HEAD_DIM), lambda b, h, i: (0, 0))
    return pl.pallas_call(
        functools.partial(_attn_kernel, lam_init=lam_init),
        grid=(B, N_HEADS // hs, S // tq),
        in_specs=[
            vec, vec, vec, vec,
            pl.BlockSpec((VDIM, 1), lambda b, h, i: (0, 0)),
            pl.BlockSpec((1, tq, hs * VDIM), lambda b, h, i: (b, i, h)),
            pl.BlockSpec((1, S, hs * VDIM), lambda b, h, i: (b, 0, h)),
            pl.BlockSpec((1, hs, nkb, VT_ROWS, tk), lambda b, h, i: (b, h, 0, 0, 0)),
            pl.BlockSpec((1, 1, 2 * hs, tq), lambda b, h, i: (b, h, 0, i)),
            pl.BlockSpec((1, 1, VDIM), lambda b, h, i: (b, 0, 0)),
        ],
        out_specs=pl.BlockSpec((1, tq, hs * VDIM), lambda b, h, i: (b, i, h)),
        out_shape=jax.ShapeDtypeStruct((B, S, ATTN_W), BF16),
        scratch_shapes=[
            pltpu.VMEM((2 * hs, 1, tq), F32),
            pltpu.VMEM((2 * hs, VT_ROWS, tq), F32),
        ],
        compiler_params=pltpu.CompilerParams(
            dimension_semantics=("arbitrary", "arbitrary", "arbitrary"),
            vmem_limit_bytes=VMEM_LIMIT),
        name="prompt_attention",
    )(*lams, subln_col, q, k, vt, qn.reshape(B, N_HEADS // hs, 2 * hs, S), kmax)


def _ff_cols(j):
    return slice(j * FF_CHUNK, (j + 1) * FF_CHUNK)


def _ffn_gate(j, a, bg, a2, a1, cw_ref, cb_ref):
    cw = cw_ref[:, _ff_cols(j)]
    c = cw[0:1] * a2 + cw[1:2] * a1 + cw[2:3] * a + cb_ref[:, _ff_cols(j)]
    return ((c * jax.nn.sigmoid(c)) * bg).astype(BF16)


def _ffn_kernel(x_ref, py_ref, ay_ref, wo_ref, g2_ref, wu_ref, cw_ref, cb_ref, wd_ref,
                fg_ref, out_ref, ctail_ref, carry, abuf, bbuf, h2_ref, acc_ref, gbuf, *, final):
    i = pl.program_id(1)
    tm = x_ref.shape[1]

    @pl.when(i == 0)
    def _():
        carry[...] = jnp.zeros(carry.shape, F32)

    x1 = (x_ref[0]
          + jnp.dot(py_ref[0], wo_ref[0:POOL_W, :], preferred_element_type=F32)
          + jnp.dot(ay_ref[0], wo_ref[POOL_W:POOL_W + ATTN_W, :], preferred_element_type=F32))
    h2_ref[...] = _rms(x1, g2_ref[...]).astype(BF16)

    def up(j):
        h2 = h2_ref[...]
        abuf[j % 2, CARRY_ROWS:CARRY_ROWS + tm, :] = jnp.dot(
            h2, wu_ref[:, _ff_cols(j)], preferred_element_type=F32)
        bbuf[j % 2] = jnp.dot(h2, wu_ref[:, _ff_cols(N_FF_CHUNKS + j)],
                              preferred_element_type=F32)

    acc_ref[...] = x1
    up(0)
    for j in range(N_FF_CHUNKS):
        if j + 1 < N_FF_CHUNKS:
            up(j + 1)
        ab = abuf.at[j % 2]
        ab[0:CARRY_ROWS, :] = carry[:, _ff_cols(j)]
        carry[:, _ff_cols(j)] = ab[tm:tm + CARRY_ROWS, :]
        gated = _ffn_gate(j, ab[CARRY_ROWS:CARRY_ROWS + tm, :], bbuf[j % 2],
                          ab[CARRY_ROWS - 2:CARRY_ROWS - 2 + tm, :],
                          ab[CARRY_ROWS - 1:CARRY_ROWS - 1 + tm, :], cw_ref, cb_ref)
        gbuf[:, _ff_cols(j)] = gated

    acc = acc_ref[...] + jnp.dot(gbuf[...], wd_ref[...], preferred_element_type=F32)
    if final:
        acc = _rms(acc, fg_ref[...])
    out_ref[0] = acc
    ctail_ref[0] = carry[...]


def _ffn(x, py, ay, w_out, g2, wu, cw, cb, wd, fg, final):
    B, S, _ = x.shape
    tm = ROW_TILE
    nb = S // tm
    row = lambda b, i: (b, i, 0)
    const2 = lambda b, i: (0, 0)
    resident = dict(pipeline_mode=pl.Buffered(1))
    return pl.pallas_call(
        functools.partial(_ffn_kernel, final=final),
        grid=(B, nb),
        in_specs=[
            pl.BlockSpec((1, tm, D_MODEL), row),
            pl.BlockSpec((1, tm, POOL_W), row),
            pl.BlockSpec((1, tm, ATTN_W), row),
            pl.BlockSpec((D_MODEL, D_MODEL), const2, **resident),
            pl.BlockSpec((1, D_MODEL), const2),
            pl.BlockSpec((D_MODEL, 2 * D_FF), const2, **resident),
            pl.BlockSpec((CONV_W, D_FF), const2),
            pl.BlockSpec((1, D_FF), const2),
            pl.BlockSpec((D_FF, D_MODEL), const2, **resident),
            pl.BlockSpec((1, D_MODEL), const2),
        ],
        out_specs=[
            pl.BlockSpec((1, tm, D_MODEL), row),
            pl.BlockSpec((1, CARRY_ROWS, D_FF), lambda b, i: (b, 0, 0)),
        ],
        out_shape=[
            jax.ShapeDtypeStruct((B, S, D_MODEL), F32),
            jax.ShapeDtypeStruct((B, CARRY_ROWS, D_FF), F32),
        ],
        scratch_shapes=[
            pltpu.VMEM((CARRY_ROWS, D_FF), F32),
            pltpu.VMEM((2, CARRY_ROWS + tm, FF_CHUNK), F32),
            pltpu.VMEM((2, tm, FF_CHUNK), F32),
            pltpu.VMEM((tm, D_MODEL), BF16),
            pltpu.VMEM((tm, D_MODEL), F32),
            pltpu.VMEM((tm, D_FF), BF16),
        ],
        compiler_params=pltpu.CompilerParams(
            dimension_semantics=("arbitrary", "arbitrary"), vmem_limit_bytes=VMEM_LIMIT),
        name="prompt_ffn",
    )(x, py, ay, w_out, g2, wu, cw, cb, wd, fg)


def _s_inproj_kernel(x_ref, g_ref, w_ref, cos_ref, sin_ref, st_ref, pw_ref, ps_ref,
                     q_ref, k_ref, v_ref, u_ref, py_ref, *, pos):
    h = _rms(x_ref[...], g_ref[...]).astype(BF16)
    proj = jnp.dot(h, w_ref[...], preferred_element_type=F32)
    u = proj[:, 0:POOL_W]
    u_ref[...] = u
    cos = cos_ref[...]
    sin = sin_ref[...]
    for hd in range(N_HEADS):
        sl = slice(hd * VDIM, (hd + 1) * VDIM)
        q_ref[:, sl] = _rope(proj[:, POOL_W + hd * VDIM:POOL_W + (hd + 1) * VDIM], cos, sin)
        k_ref[:, sl] = _rope(
            proj[:, POOL_W + ATTN_W + hd * VDIM:POOL_W + ATTN_W + (hd + 1) * VDIM], cos, sin)
    v_ref[...] = proj[:, POOL_W + 2 * ATTN_W:IN_W]
    for g, w in enumerate(POOL_WINDOWS):
        sl = slice(g * POOL_GC, (g + 1) * POOL_GC)
        ug = u[:, sl]
        acc = ug
        for d in range(1, w):
            acc = acc + st_ref[POOL_HIST - d, :, sl]
        cnt = float(min(pos + 1, w))
        dlt = (acc / cnt - ug).astype(BF16)
        y = jnp.dot(dlt, pw_ref[g], preferred_element_type=F32) * ps_ref[:, sl]
        py_ref[:, sl] = y.astype(BF16)


def _s_inproj(x, g1, w_in, cos, sin, state_t, pool_w, pool_scale, pos):
    db = x.shape[0]
    return pl.pallas_call(
        functools.partial(_s_inproj_kernel, pos=pos),
        out_shape=[
            jax.ShapeDtypeStruct((db, ATTN_W), F32),
            jax.ShapeDtypeStruct((db, ATTN_W), F32),
            jax.ShapeDtypeStruct((db, ATTN_W), F32),
            jax.ShapeDtypeStruct((db, POOL_W), F32),
            jax.ShapeDtypeStruct((db, POOL_W), BF16),
        ],
        compiler_params=pltpu.CompilerParams(vmem_limit_bytes=VMEM_LIMIT),
        name="sample_inproj",
    )(x, g1, w_in, cos, sin, state_t, pool_w, pool_scale)


def _s_attn_kernel(pt_ref, lq1, lk1, lq2, lk2, g_ref, q_ref, kn_ref, vn_ref, *rest, lam_init):
    np_ = PAGES_PER_STEP
    nbs = DECODE_BATCH_PER_STEP
    k_refs = rest[:nbs * np_]
    v_refs = rest[nbs * np_:2 * nbs * np_]
    o_ref, m_ref, l_ref, acc_ref = rest[2 * nbs * np_:]
    j = pl.program_id(1)
    rows = SCORE_ROWS
    prow = k_refs[0].shape[0]

    @pl.when(j == 0)
    def _():
        m_ref[...] = jnp.full(m_ref.shape, NEG_BIG, F32)
        l_ref[...] = jnp.zeros(l_ref.shape, F32)
        acc_ref[...] = jnp.zeros(acc_ref.shape, F32)

    r = lax.broadcasted_iota(jnp.int32, (rows, VDIM), 0)
    lane = lax.broadcasted_iota(jnp.int32, (rows, VDIM), 1)
    rr = lax.broadcasted_iota(jnp.int32, (rows, prow), 0)
    col = lax.broadcasted_iota(jnp.int32, (rows, prow), 1)
    own = (col % N_HEADS) == (rr // 2)
    qm = [jnp.where((lane // HEAD_DIM) == (r % 2), q_ref[bi] * Q_SCALE, 0.0).astype(BF16)
          for bi in range(nbs)]

    s = [[jnp.where(own, lax.dot_general(qm[bi], kr[...].astype(BF16), (((1,), (1,)), ((), ())),
                                         preferred_element_type=F32), NEG_BIG)
          for kr in k_refs[bi * np_:(bi + 1) * np_]] for bi in range(nbs)]
    for bi in range(nbs):
        m_blk = s[bi][0]
        for sp in s[bi][1:]:
            m_blk = jnp.maximum(m_blk, sp)
        m_old = m_ref[bi]
        m_new = jnp.maximum(m_old, jnp.max(m_blk, axis=1, keepdims=True))
        alpha = jnp.exp2(m_old - m_new)
        p = [jnp.exp2(sp - m_new).astype(BF16) for sp in s[bi]]
        l_blk = p[0].astype(F32)
        for pp in p[1:]:
            l_blk = l_blk + pp.astype(F32)
        l_ref[bi] = alpha * l_ref[bi] + jnp.sum(l_blk, axis=1, keepdims=True)
        pv = jnp.zeros((rows, VDIM), F32)
        for pp, vr in zip(p, v_refs[bi * np_:(bi + 1) * np_]):
            pv = pv + jnp.dot(pp, vr[...].astype(BF16), preferred_element_type=F32)
        acc_ref[bi] = alpha * acc_ref[bi] + pv
        m_ref[bi] = m_new

    @pl.when(j == pl.num_programs(1) - 1)
    def _():
        lam = _lam(lq1, lk1, lq2, lk2, lam_init)
        for bi in range(nbs):
            kn = kn_ref[bi].astype(BF16).astype(F32)
            vn = vn_ref[bi].astype(BF16).astype(F32)
            s_new = jnp.sum(qm[bi].astype(F32) * kn, axis=1, keepdims=True)
            m_o = m_ref[bi]
            m_n = jnp.maximum(m_o, s_new)
            a_n = jnp.exp2(m_o - m_n)
            p_n = jnp.exp2(s_new - m_n)
            l_n = a_n * l_ref[bi] + p_n
            on = (a_n * acc_ref[bi] + p_n.astype(BF16).astype(F32) * vn) / l_n
            o = jnp.concatenate(
                [on[2 * hd:2 * hd + 1, :] - lam * on[2 * hd + 1:2 * hd + 2, :]
                 for hd in range(N_HEADS)], axis=0)
            y = o * lax.rsqrt(jnp.mean(o * o, axis=-1, keepdims=True) + EPS) * g_ref[...]
            o_ref[bi] = y * (1.0 - lam_init)


def _s_attention(page_table, lams, subln_row, q16, kn16, vn16, cache_k, cache_v, layer, lam_init):
    db, n_pages = page_table.shape
    np_ = PAGES_PER_STEP
    nbs = DECODE_BATCH_PER_STEP
    prow = cache_k.shape[2]
    rows = SCORE_ROWS
    vec = pl.BlockSpec((1, HEAD_DIM), lambda g, j, pt: (0, 0))
    tok = pl.BlockSpec((nbs, rows, VDIM), lambda g, j, pt: (g, 0, 0))

    def page_spec(bi, idx):
        return pl.BlockSpec((None, None, prow, VDIM),
                            lambda g, j, pt: (layer, pt[g * nbs + bi, j * np_ + idx], 0, 0))

    pages = [page_spec(bi, idx) for bi in range(nbs) for idx in range(np_)]
    n_in = nbs * np_
    grid_spec = pltpu.PrefetchScalarGridSpec(
        num_scalar_prefetch=1,
        grid=(db // nbs, n_pages // np_),
        in_specs=[vec, vec, vec, vec,
                  pl.BlockSpec((1, VDIM), lambda g, j, pt: (0, 0)),
                  tok, tok, tok] + pages + pages,
        out_specs=pl.BlockSpec((nbs, N_HEADS, VDIM), lambda g, j, pt: (g, 0, 0)),
        scratch_shapes=[
            pltpu.VMEM((nbs, rows, 1), F32),
            pltpu.VMEM((nbs, rows, 1), F32),
            pltpu.VMEM((nbs, rows, VDIM), F32),
        ],
    )
    return pl.pallas_call(
        functools.partial(_s_attn_kernel, lam_init=lam_init),
        grid_spec=grid_spec,
        out_shape=jax.ShapeDtypeStruct((db, N_HEADS, VDIM), F32),
        compiler_params=pltpu.CompilerParams(
            dimension_semantics=("arbitrary", "arbitrary"), vmem_limit_bytes=VMEM_LIMIT),
        name="sample_attention",
    )(page_table, *lams, subln_row, q16, kn16, vn16, *([cache_k] * n_in), *([cache_v] * n_in))


def _s_ffn_kernel(x_ref, py_ref, ay_ref, wo_ref, g2_ref, wu_ref, cw_ref, cb_ref, wd_ref,
                  fg_ref, prev_ref, out_ref, a_ref, gbuf, *, final):
    x1 = (x_ref[...]
          + jnp.dot(py_ref[...], wo_ref[0:POOL_W, :], preferred_element_type=F32)
          + jnp.dot(ay_ref[...].astype(BF16), wo_ref[POOL_W:POOL_W + ATTN_W, :],
                    preferred_element_type=F32))
    h2 = _rms(x1, g2_ref[...]).astype(BF16)
    for j in range(N_FF_CHUNKS):
        a = jnp.dot(h2, wu_ref[:, _ff_cols(j)], preferred_element_type=F32)
        bg = jnp.dot(h2, wu_ref[:, _ff_cols(N_FF_CHUNKS + j)], preferred_element_type=F32)
        a_ref[:, _ff_cols(j)] = a
        gbuf[:, _ff_cols(j)] = _ffn_gate(j, a, bg, prev_ref[0, :, _ff_cols(j)],
                                         prev_ref[1, :, _ff_cols(j)], cw_ref, cb_ref)

    out = x1 + jnp.dot(gbuf[...], wd_ref[...], preferred_element_type=F32)
    if final:
        out = _rms(out, fg_ref[...])
    out_ref[...] = out


def _s_ffn(x, py, ay, w_out, g2, wu, cw, cb, wd, fg, prev, final):
    db = x.shape[0]
    return pl.pallas_call(
        functools.partial(_s_ffn_kernel, final=final),
        out_shape=[
            jax.ShapeDtypeStruct((db, D_MODEL), F32),
            jax.ShapeDtypeStruct((db, D_FF), F32),
        ],
        scratch_shapes=[pltpu.VMEM((db, D_FF), BF16)],
        compiler_params=pltpu.CompilerParams(vmem_limit_bytes=VMEM_LIMIT),
        name="sample_ffn",
    )(x, py, ay, w_out, g2, wu, cw, cb, wd, fg, prev)


def _rope_tables(pos):
    half = HEAD_DIM // 2
    inv = 1.0 / (ROPE_THETA ** (jnp.arange(half, dtype=F32) * (2.0 / HEAD_DIM)))
    ang = pos.astype(F32)[:, None] * inv[None, :]
    cos = jnp.cos(ang)
    sin = jnp.sin(ang)
    return (jnp.concatenate([cos, cos, cos, cos], axis=1),
            jnp.concatenate([-sin, sin, -sin, sin], axis=1))


def _score_rows(x):
    db = x.shape[0]
    x = jnp.repeat(x.reshape(db, N_HEADS, VDIM), 2, axis=1)
    return jnp.pad(x, ((0, 0), (0, SCORE_ROWS - 2 * N_HEADS), (0, 0)))


def kernel(x_prompt, x_sample, cache_k, cache_v, state_pool, state_conv, page_table, norm1_g, w_in, pool_w, pool_scale, lam_q1, lam_k1, lam_q2, lam_k2, subln_g, w_out, norm2_g, w_up, conv_w, conv_b, w_down, final_g):
    B, S, _ = x_prompt.shape
    DB, T, _ = x_sample.shape
    depth = w_in.shape[0]
    n_pool, page = cache_k.shape[1], cache_k.shape[2]
    past = page_table.shape[1] * page
    assert T == 1 and S % ROW_TILE == 0 and page_table.shape[1] % PAGES_PER_STEP == 0

    cos_p, sin_p = _rope_tables(jnp.arange(S, dtype=jnp.int32))
    cos_s, sin_s = _rope_tables(past + jnp.arange(T, dtype=jnp.int32))
    ck = cache_k.reshape(depth, n_pool, page * N_HEADS, VDIM)
    cv = cache_v.reshape(depth, n_pool, page * N_HEADS, VDIM)
    fg = final_g.reshape(1, D_MODEL)

    xp = x_prompt
    xs = x_sample.reshape(DB, D_MODEL)
    cache_rows = None
    outs = [[] for _ in range(6)]
    for l in range(depth):
        li = _lambda_init(l)
        final = l == depth - 1
        g1 = norm1_g[l].reshape(1, D_MODEL)
        g2 = norm2_g[l].reshape(1, D_MODEL)
        w_in_b = w_in[l].astype(BF16)
        pool_w_b = pool_w[l].astype(BF16)
        ps = pool_scale[l].reshape(1, POOL_W)
        lams = [v[l].reshape(1, HEAD_DIM) for v in (lam_q1, lam_k1, lam_q2, lam_k2)]
        w_out_b = w_out[l].astype(BF16)
        wu = w_up[l].astype(BF16)
        cw = conv_w[l]
        cb = conv_b[l].reshape(1, D_FF)
        wd = w_down[l].astype(BF16)

        q, k, vt, kf, vf, py, tail, qn, kmax = _inproj(xp, g1, w_in_b, cos_p, sin_p, pool_w_b,
                                                       ps, l, depth, cache_rows)
        cache_rows = (kf, vf)
        ay = _attention(q, k, vt, qn, kmax, lams, subln_g[l].reshape(VDIM, 1), li)
        xp, ctail = _ffn(xp, py, ay, w_out_b, g2, wu, cw, cb, wd, fg, final)
        outs[0].append(tail[:, POOL_CARRY - POOL_HIST:])
        outs[1].append(ctail[:, CARRY_ROWS - (CONV_W - 1):, :])

        state_t = state_pool[l].transpose(1, 0, 2)
        qs, ksn, vsn, us, pys = _s_inproj(xs, g1, w_in_b, cos_s, sin_s, state_t, pool_w_b, ps, past)
        ays = _s_attention(page_table, lams, subln_g[l].reshape(1, VDIM),
                           _score_rows(qs), _score_rows(ksn), _score_rows(vsn), ck, cv, l, li)
        prev = state_conv[l].transpose(1, 0, 2)
        xs, a_s = _s_ffn(xs, pys, ays.reshape(DB, ATTN_W), w_out_b, g2, wu, cw, cb, wd, fg,
                         prev, final)
        a_rows = a_s[:, None, :]
        outs[2].append(ksn.reshape(DB, T, N_HEADS, VDIM))
        outs[3].append(vsn.reshape(DB, T, N_HEADS, VDIM))
        outs[4].append(jnp.concatenate([state_pool[l][:, 1:], us[:, None, :]], axis=1))
        outs[5].append(jnp.concatenate([state_conv[l][:, 1:], a_rows], axis=1))

    kf, vf = cache_rows
    k_new_prompt = kf.reshape(depth, B, S // page, page, N_HEADS, VDIM)
    v_new_prompt = vf.reshape(depth, B, S // page, page, N_HEADS, VDIM)
    y_prompt = xp
    y_sample = xs.reshape(DB, T, D_MODEL)
    stk = [jnp.stack(o) for o in outs]
    return (y_prompt, y_sample, k_new_prompt, v_new_prompt, stk[0], stk[1],
            stk[2], stk[3], stk[4], stk[5])
```
